```python
import jax
import jax.numpy as jnp
from jax import lax
import numpy as np

D_MODEL = 1024
BATCH = 8
SEQ = 2048
DEPTH = 4
DEC_BATCH = 2
DEC_SEQ = 16384
PAST_LEN = 128

N_AB_LAYERS = (DEPTH + 1) // 2
N_C_LAYERS = DEPTH // 2
EPS = 1e-6
ROPE_BASE = 10000.0

RET_HEADS = 4
RET_DK = 64
RET_DV = 128
RET_CHUNK = 128
MLA_HEADS = 8
MLA_NOPE = 64
MLA_ROPE = 32
MLA_V = 64
MLA_Q_LORA = 256
MLA_KV_LORA = 128
MLA_QBLOCK = 128
AB_IN = 2 * RET_HEADS * RET_DK + 2 * RET_HEADS * RET_DV + MLA_Q_LORA + MLA_KV_LORA + MLA_ROPE
AB_MIX = RET_HEADS * RET_DV + MLA_HEADS * MLA_V
D_RNN = D_MODEL
RG_BLOCKS = 8
RG_BLK = D_RNN // RG_BLOCKS
CONV_W = 4
CONV_LEFT = CONV_W // 2
RG_C = 8.0
N_EXPERTS = 16
EC_FACTOR = 2
D_EXPERT = 2 * D_MODEL

kernel_name = 'hybrid_ret_mla_rglru_ec_encoder'


def rmsnorm(x, g):
    xf = x.astype(jnp.float32)
    y = xf * lax.rsqrt(jnp.mean(xf * xf, axis=-1, keepdims=True) + EPS)
    return (y * g.astype(jnp.float32)).astype(x.dtype)


def rope_tables(T, dim):
    inv = 1.0 / (ROPE_BASE ** (jnp.arange(0, dim, 2, dtype=jnp.float32) / dim))
    ang = jnp.arange(T, dtype=jnp.float32)[:, None] * inv[None, :]
    return jnp.cos(ang), jnp.sin(ang)


def apply_rope(x, cos, sin):
    shape = (x.shape[1],) + (1,) * (x.ndim - 3) + (cos.shape[-1],)
    c = cos.reshape(shape)
    s = sin.reshape(shape)
    x1, x2 = jnp.split(x.astype(jnp.float32), 2, axis=-1)
    return jnp.concatenate([x1 * c - x2 * s, x1 * s + x2 * c], axis=-1).astype(x.dtype)


def retention_bidir(q, k, v, decay_logit):
    B, T, H, dk = q.shape
    dv = v.shape[-1]
    Cs = RET_CHUNK
    NC = T // Cs
    log_g = jax.nn.log_sigmoid(decay_logit.astype(jnp.float32))
    lgf, lgb = log_g[0], log_g[1]
    pos = jnp.arange(Cs, dtype=jnp.float32)
    delta = pos[:, None] - pos[None, :]
    rate = jnp.where(delta[None] >= 0, lgf[:, None, None], lgb[:, None, None])
    intra = jnp.exp(jnp.abs(delta)[None] * rate)
    qc = q.reshape(B, NC, Cs, H, dk)
    kc = k.reshape(B, NC, Cs, H, dk)
    vc = v.reshape(B, NC, Cs, H, dv)
    s = jnp.einsum('bnihd,bnjhd->bnhij', qc, kc) * intra
    o = jnp.einsum('bnhij,bnjhe->bnihe', s, vc)
    wf_k = jnp.exp((Cs - 1.0 - pos)[:, None] * lgf[None])
    wf_q = jnp.exp((pos + 1.0)[:, None] * lgf[None])
    wb_k = jnp.exp(pos[:, None] * lgb[None])
    wb_q = jnp.exp((Cs - pos)[:, None] * lgb[None])
    kv_f = jnp.einsum('bnjhd,bnjhe,jh->nbhde', kc, vc, wf_k)
    kv_b = jnp.einsum('bnjhd,bnjhe,jh->nbhde', kc, vc, wb_k)
    dec_f = jnp.exp(Cs * lgf)[None, :, None, None]
    dec_b = jnp.exp(Cs * lgb)[None, :, None, None]
    S0 = jnp.zeros((B, H, dk, dv), jnp.float32)

    def step_f(S, kv):
        return S * dec_f + kv, S

    def step_b(S, kv):
        return S * dec_b + kv, S

    _, Sf = lax.scan(step_f, S0, kv_f.astype(jnp.float32))
    _, Sb = lax.scan(step_b, S0, kv_b.astype(jnp.float32), reverse=True)
    o = (o + jnp.einsum('bnihd,nbhde,ih->bnihe', qc, Sf, wf_q)
         + jnp.einsum('bnihd,nbhde,ih->bnihe', qc, Sb, wb_q))
    return o.reshape(B, T, H, dv)


def mla_attention(q_nope, q_rope, k_nope, k_rope, v):
    B, T, H, _ = q_nope.shape
    nq = T // MLA_QBLOCK
    scale = (MLA_NOPE + MLA_ROPE) ** -0.5

    def block(qs):
        qn, qr = qs
        s = (jnp.einsum('bqhd,bkhd->bhqk', qn, k_nope)
             + jnp.einsum('bqhr,bkr->bhqk', qr, k_rope))
        p = jax.nn.softmax(s.astype(jnp.float32) * scale, axis=-1)
        return jnp.einsum('bhqk,bkhe->bqhe', p.astype(v.dtype), v)

    qn_b = q_nope.reshape(B, nq, MLA_QBLOCK, H, MLA_NOPE).transpose(1, 0, 2, 3, 4)
    qr_b = q_rope.reshape(B, nq, MLA_QBLOCK, H, MLA_ROPE).transpose(1, 0, 2, 3, 4)
    o = lax.map(block, (qn_b, qr_b))
    return o.transpose(1, 0, 2, 3, 4).reshape(B, T, H, MLA_V)


def ab_mixer(h, w_in, ret_decay_logit, ret_gn, q_norm, w_uq, kv_norm, w_ukv, w_out):
    B, T, _ = h.shape
    u = h @ w_in
    sizes = (RET_HEADS * RET_DK, RET_HEADS * RET_DK, RET_HEADS * RET_DV, RET_HEADS * RET_DV,
             MLA_Q_LORA, MLA_KV_LORA)
    cuts = [int(c) for c in np.cumsum(sizes)]
    q_r, k_r, v_r, g_r, c_q, c_kv, k_pe = jnp.split(u, cuts, axis=-1)
    cos_r, sin_r = rope_tables(T, RET_DK)
    q = apply_rope(q_r.reshape(B, T, RET_HEADS, RET_DK), cos_r, sin_r)
    k = apply_rope(k_r.reshape(B, T, RET_HEADS, RET_DK), cos_r, sin_r) * (RET_DK ** -0.5)
    v = v_r.reshape(B, T, RET_HEADS, RET_DV)
    o = retention_bidir(q, k, v, ret_decay_logit).astype(jnp.float32)
    o = o * lax.rsqrt(jnp.mean(o * o, axis=-1, keepdims=True) + EPS)
    y_a = (o.reshape(B, T, RET_HEADS * RET_DV) * ret_gn.astype(jnp.float32)
           * jax.nn.silu(g_r.astype(jnp.float32))).astype(h.dtype)
    cos_m, sin_m = rope_tables(T, MLA_ROPE)
    qh = (rmsnorm(c_q, q_norm) @ w_uq).reshape(B, T, MLA_HEADS, MLA_NOPE + MLA_ROPE)
    q_nope, q_pe = jnp.split(qh, [MLA_NOPE], axis=-1)
    q_pe = apply_rope(q_pe, cos_m, sin_m)
    kvh = (rmsnorm(c_kv, kv_norm) @ w_ukv).reshape(B, T, MLA_HEADS, MLA_NOPE + MLA_V)
    k_nope, v_m = jnp.split(kvh, [MLA_NOPE], axis=-1)
    k_pe = apply_rope(k_pe, cos_m, sin_m)
    y_b = mla_attention(q_nope, q_pe, k_nope, k_pe, v_m).reshape(B, T, MLA_HEADS * MLA_V)
    return jnp.concatenate([y_a, y_b.astype(h.dtype)], axis=-1) @ w_out


def _lin_combine(left, right):
    a1, b1 = left
    a2, b2 = right
    return a1 * a2, a2 * b1 + b2


def rglru_mixer(h, w_in, conv_w, conv_b, wa, ba, wx, bx, lam, w_out):
    B, T, _ = h.shape
    u = h @ w_in
    gate, xr = jnp.split(u, 2, axis=-1)
    xp = jnp.pad(xr, ((0, 0), (CONV_LEFT, CONV_W - 1 - CONV_LEFT), (0, 0)))
    xc = conv_b + xp[:, 0:T] * conv_w[0]
    for j in range(1, CONV_W):
        xc = xc + xp[:, j:j + T] * conv_w[j]
    xb = xc.reshape(B, T, RG_BLOCKS, RG_BLK)
    hsum = jnp.zeros((B, T, D_RNN), jnp.float32)
    for d in range(2):
        r = jax.nn.sigmoid(jnp.einsum('btnc,ncd->btnd', xb, wa[d]).reshape(B, T, D_RNN) + ba[d])
        i = jax.nn.sigmoid(jnp.einsum('btnc,ncd->btnd', xb, wx[d]).reshape(B, T, D_RNN) + bx[d])
        log_a = -RG_C * r.astype(jnp.float32) * jax.nn.softplus(-lam[d].astype(jnp.float32))
        a = jnp.exp(log_a)
        b = jnp.sqrt(-jnp.expm1(2.0 * log_a)) * (i * xc).astype(jnp.float32)
        _, hs = lax.associative_scan(_lin_combine, (a, b), axis=1, reverse=(d == 1))
        hsum = hsum + hs
    y = jax.nn.gelu(gate.astype(jnp.float32)) * hsum
    return y.astype(h.dtype) @ w_out


def ec_moe(h, router, w_gate, w_up, w_down):
    B, T, D = h.shape
    N = B * T
    cap = max(1, EC_FACTOR * N // N_EXPERTS)
    xf = h.reshape(N, D)
    aff = jax.nn.softmax((xf @ router).astype(jnp.float32), axis=-1)
    g, idx = lax.top_k(aff.T, cap)
    xg = jnp.take(xf, idx, axis=0)
    hid = jax.nn.silu(jnp.einsum('ecd,edf->ecf', xg, w_gate)) * jnp.einsum('ecd,edf->ecf', xg, w_up)
    out = jnp.einsum('ecf,efd->ecd', hid, w_down) * g[..., None].astype(h.dtype)
    y = jnp.zeros((N, D), h.dtype).at[idx.reshape(-1)].add(out.reshape(-1, D).astype(h.dtype))
    return y.reshape(B, T, D)


def trunk(x, norm_mix, norm_ffn, norm_final, ab_w_in, ret_decay_logit, ret_gn, mla_q_norm,
          mla_w_uq, mla_kv_norm, mla_w_ukv, ab_w_out, rg_w_in, rg_conv_w, rg_conv_b, rg_wa, rg_ba,
          rg_wx, rg_bx, rg_lambda, rg_w_out, moe_router, moe_w_gate, moe_w_up, moe_w_down):
    for l in range(DEPTH):
        h = rmsnorm(x, norm_mix[l])
        j = l // 2
        if l % 2 == 0:
            x = x + ab_mixer(h, ab_w_in[j], ret_decay_logit[j], ret_gn[j], mla_q_norm[j], mla_w_uq[j],
                             mla_kv_norm[j], mla_w_ukv[j], ab_w_out[j])
        else:
            x = x + rglru_mixer(h, rg_w_in[j], rg_conv_w[j], rg_conv_b[j], rg_wa[j], rg_ba[j],
                                rg_wx[j], rg_bx[j], rg_lambda[j], rg_w_out[j])
        x = x + ec_moe(rmsnorm(x, norm_ffn[l]), moe_router[l], moe_w_gate[l], moe_w_up[l], moe_w_down[l])
    return rmsnorm(x, norm_final)


def setup_inputs(seed: int = 0) -> dict:
    key = jax.random.key(seed)
    ks = jax.random.split(key, 32)
    f32 = jnp.float32

    def nrm(k, shape, scale):
        return jax.random.normal(k, shape, f32) * scale

    x_prompt = nrm(ks[0], (BATCH, SEQ, D_MODEL), 1.0)
    x_sample = nrm(ks[1], (DEC_BATCH, DEC_SEQ, D_MODEL), 1.0)
    norm_mix = 1.0 + nrm(ks[2], (DEPTH, D_MODEL), 0.02)
    norm_ffn = 1.0 + nrm(ks[3], (DEPTH, D_MODEL), 0.02)
    norm_final = 1.0 + nrm(ks[4], (D_MODEL,), 0.02)
    ab_w_in = nrm(ks[5], (N_AB_LAYERS, D_MODEL, AB_IN), D_MODEL ** -0.5)
    base = 1.0 - 2.0 ** (-5.0 - jnp.arange(RET_HEADS, dtype=f32))
    ret_decay_logit = ((jnp.log(base) - jnp.log1p(-base))[None, None, :]
                       + nrm(ks[6], (N_AB_LAYERS, 2, RET_HEADS), 0.1))
    ret_gn = 1.0 + nrm(ks[7], (N_AB_LAYERS, RET_HEADS * RET_DV), 0.02)
    mla_q_norm = 1.0 + nrm(ks[8], (N_AB_LAYERS, MLA_Q_LORA), 0.02)
    mla_w_uq = nrm(ks[9], (N_AB_LAYERS, MLA_Q_LORA, MLA_HEADS * (MLA_NOPE + MLA_ROPE)), MLA_Q_LORA ** -0.5)
    mla_kv_norm = 1.0 + nrm(ks[10], (N_AB_LAYERS, MLA_KV_LORA), 0.02)
    mla_w_ukv = nrm(ks[11], (N_AB_LAYERS, MLA_KV_LORA, MLA_HEADS * (MLA_NOPE + MLA_V)), MLA_KV_LORA ** -0.5)
    ab_w_out = nrm(ks[12], (N_AB_LAYERS, AB_MIX, D_MODEL), AB_MIX ** -0.5)
    rg_w_in = nrm(ks[13], (N_C_LAYERS, D_MODEL, 2 * D_RNN), D_MODEL ** -0.5)
    rg_conv_w = nrm(ks[14], (N_C_LAYERS, CONV_W, D_RNN), CONV_W ** -0.5)
    rg_conv_b = nrm(ks[15], (N_C_LAYERS, D_RNN), 0.01)
    rg_wa = nrm(ks[16], (N_C_LAYERS, 2, RG_BLOCKS, RG_BLK, RG_BLK), RG_BLK ** -0.5)
    rg_ba = nrm(ks[17], (N_C_LAYERS, 2, D_RNN), 0.01)
    rg_wx = nrm(ks[18], (N_C_LAYERS, 2, RG_BLOCKS, RG_BLK, RG_BLK), RG_BLK ** -0.5)
    rg_bx = nrm(ks[19], (N_C_LAYERS, 2, D_RNN), 0.01)
    u = jax.random.uniform(ks[20], (N_C_LAYERS, 2, D_RNN), f32, 0.9, 0.999)
    a_base = u ** (1.0 / RG_C)
    rg_lambda = jnp.log(a_base) - jnp.log1p(-a_base)
    rg_w_out = nrm(ks[21], (N_C_LAYERS, D_RNN, D_MODEL), D_RNN ** -0.5)
    moe_router = nrm(ks[22], (DEPTH, D_MODEL, N_EXPERTS), D_MODEL ** -0.5)
    moe_w_gate = nrm(ks[23], (DEPTH, N_EXPERTS, D_MODEL, D_EXPERT), D_MODEL ** -0.5)
    moe_w_up = nrm(ks[24], (DEPTH, N_EXPERTS, D_MODEL, D_EXPERT), D_MODEL ** -0.5)
    moe_w_down = nrm(ks[25], (DEPTH, N_EXPERTS, D_EXPERT, D_MODEL), D_EXPERT ** -0.5)
    return {'x_prompt': x_prompt, 'x_sample': x_sample, 'norm_mix': norm_mix, 'norm_ffn': norm_ffn,
            'norm_final': norm_final, 'ab_w_in': ab_w_in, 'ret_decay_logit': ret_decay_logit,
            'ret_gn': ret_gn, 'mla_q_norm': mla_q_norm, 'mla_w_uq': mla_w_uq, 'mla_kv_norm': mla_kv_norm,
            'mla_w_ukv': mla_w_ukv, 'ab_w_out': ab_w_out, 'rg_w_in': rg_w_in, 'rg_conv_w': rg_conv_w,
            'rg_conv_b': rg_conv_b, 'rg_wa': rg_wa, 'rg_ba': rg_ba, 'rg_wx': rg_wx, 'rg_bx': rg_bx,
            'rg_lambda': rg_lambda, 'rg_w_out': rg_w_out, 'moe_router': moe_router,
            'moe_w_gate': moe_w_gate, 'moe_w_up': moe_w_up, 'moe_w_down': moe_w_down}


def reference(x_prompt, x_sample, norm_mix, norm_ffn, norm_final, ab_w_in, ret_decay_logit, ret_gn,
              mla_q_norm, mla_w_uq, mla_kv_norm, mla_w_ukv, ab_w_out, rg_w_in, rg_conv_w, rg_conv_b,
              rg_wa, rg_ba, rg_wx, rg_bx, rg_lambda, rg_w_out, moe_router, moe_w_gate, moe_w_up,
              moe_w_down):
    y_prompt = trunk(x_prompt, norm_mix, norm_ffn, norm_final, ab_w_in, ret_decay_logit, ret_gn,
                     mla_q_norm, mla_w_uq, mla_kv_norm, mla_w_ukv, ab_w_out, rg_w_in, rg_conv_w,
                     rg_conv_b, rg_wa, rg_ba, rg_wx, rg_bx, rg_lambda, rg_w_out, moe_router,
                     moe_w_gate, moe_w_up, moe_w_down)
    y_sample = trunk(x_sample, norm_mix, norm_ffn, norm_final, ab_w_in, ret_decay_logit, ret_gn,
                     mla_q_norm, mla_w_uq, mla_kv_norm, mla_w_ukv, ab_w_out, rg_w_in, rg_conv_w,
                     rg_conv_b, rg_wa, rg_ba, rg_wx, rg_bx, rg_lambda, rg_w_out, moe_router,
                     moe_w_gate, moe_w_up, moe_w_down)
    return (y_prompt, y_sample)
```

```python
import functools
import math

import jax
import jax.numpy as jnp
from jax import lax
from jax.experimental import pallas as pl
from jax.experimental.pallas import tpu as pltpu

F32 = jnp.float32
BF16 = jnp.bfloat16
I32 = jnp.int32

D_MODEL = 1024
DEPTH = 4
EPS = 1e-6
ROPE_BASE = 10000.0
RET_HEADS = 4
RET_DK = 64
RET_DV = 128
MLA_HEADS = 8
MLA_NOPE = 64
MLA_ROPE = 32
MLA_V = 64
MLA_Q_LORA = 256
MLA_KV_LORA = 128
D_RNN = D_MODEL
RG_BLOCKS = 8
RG_BLK = D_RNN // RG_BLOCKS
CONV_W = 4
CONV_LEFT = CONV_W // 2
RG_C = 8.0
N_EXPERTS = 16
EC_FACTOR = 2
D_EXPERT = 2 * D_MODEL

LANES = 128
SLOT_ALIGN = 64
SLOT_WIN = 128
VMEM_LIMIT = 56 * 1024 * 1024


def _params(sem):
    return pltpu.CompilerParams(dimension_semantics=sem, vmem_limit_bytes=VMEM_LIMIT)


def _rms(x, g):
    return x * lax.rsqrt(jnp.mean(x * x, axis=-1, keepdims=True) + EPS) * g


def _dot(a, b):
    return jnp.dot(a, b, preferred_element_type=F32)


def _dot_nt(a, b):
    return lax.dot_general(a, b, (((1,), (1,)), ((), ())), preferred_element_type=F32)


def _dot_tn(a, b):
    return lax.dot_general(a, b, (((0,), (0,)), ((), ())), preferred_element_type=F32)


AB_SEG = {
    "q": (0, 512), "q_rot": (512, 1024), "k": (1024, 1536), "k_rot": (1536, 2048),
    "v": (2048, 2560), "g": (2560, 3072), "c_q": (3072, 3328), "c_kv": (3328, 3456),
    "k_pe": (3456, 3584), "k_pe_rot": (3584, 3712),
}
AB_EXT = 3712


def _ab_in_kernel(x_ref, g_ref, w_ref, wuq_ref, wukv_ref, qn_ref, kvn_ref, cr_ref, sr_ref,
                  cq_ref, sq_ref, ck_ref, sk_ref,
                  qr_out, kr_out, vr_out, gr_out, q_out, k_out, v_out):
    hb = _rms(x_ref[...], g_ref[...]).astype(BF16)

    def seg(name):
        a, b = AB_SEG[name]
        return _dot(hb, w_ref[:, a:b])

    cr = jnp.tile(cr_ref[...], (1, RET_HEADS))
    sr = jnp.tile(sr_ref[...], (1, RET_HEADS))
    qr_out[...] = (seg("q") * cr + seg("q_rot") * sr).astype(BF16)
    kr_out[...] = (seg("k") * cr + seg("k_rot") * sr).astype(BF16)
    vr_out[...] = seg("v").astype(BF16)
    gr_out[...] = seg("g").astype(BF16)

    cqn = _rms(seg("c_q"), qn_ref[...]).astype(BF16)
    hq = MLA_HEADS * LANES
    qa = _dot(cqn, wuq_ref[:, 0:hq])
    qb = _dot(cqn, wuq_ref[:, hq:2 * hq])
    q_out[...] = (qa * jnp.tile(cq_ref[...], (1, MLA_HEADS))
                  + qb * jnp.tile(sq_ref[...], (1, MLA_HEADS))).astype(BF16)

    ckvn = _rms(seg("c_kv"), kvn_ref[...]).astype(BF16)
    kn = _dot(ckvn, wukv_ref[:, 0:hq])
    kpe = seg("k_pe") * ck_ref[...] + seg("k_pe_rot") * sk_ref[...]
    k_out[...] = (kn + jnp.tile(kpe, (1, MLA_HEADS))).astype(BF16)
    v_out[...] = _dot(ckvn, wukv_ref[:, hq:2 * hq]).astype(BF16)


def _ab_in(x, g, w_ext, wuq, wukv, qn, kvn, tabs, T, tm=256):
    n, d = x.shape
    nt = T // tm
    row = lambda i: (i, 0)
    const = lambda i: (0, 0)
    tab = lambda i: (i % nt, 0)
    hq = MLA_HEADS * LANES
    hr = RET_HEADS * LANES
    outs = [jax.ShapeDtypeStruct((n, w), BF16) for w in (hr, hr, hr, hr, hq, hq, hq)]
    return pl.pallas_call(
        _ab_in_kernel,
        grid=(n // tm,),
        in_specs=[
            pl.BlockSpec((tm, d), row),
            pl.BlockSpec((1, d), const),
            pl.BlockSpec((d, AB_EXT), const),
            pl.BlockSpec((MLA_Q_LORA, 2 * hq), const),
            pl.BlockSpec((MLA_KV_LORA, 2 * hq), const),
            pl.BlockSpec((1, MLA_Q_LORA), const),
            pl.BlockSpec((1, MLA_KV_LORA), const),
        ] + [pl.BlockSpec((tm, LANES), tab)] * 6,
        out_specs=[pl.BlockSpec((tm, w.shape[1]), row) for w in outs],
        out_shape=outs,
        compiler_params=_params(("arbitrary",)),
    )(x, g, w_ext, wuq, wukv, qn, kvn, *tabs)


def _ret_kernel(lg_ref, q_ref, k_ref, v_ref, g_ref, gn_ref, o_ref, s_ref, sb_ref, *, cs, nc):
    h = pl.program_id(1)
    p = pl.program_id(2)
    c = pl.program_id(3)
    lgf = lg_ref[0, h]
    lgb = lg_ref[1, h]
    pos = lax.broadcasted_iota(I32, (cs, LANES), 0).astype(F32)
    k = k_ref[...]
    v = v_ref[...]
    kf = k.astype(F32)

    @pl.when(c == 0)
    def _():
        s_ref[...] = jnp.zeros_like(s_ref)

    def decay(lg):
        return jnp.exp(jnp.full((1, LANES), cs, F32) * lg)

    @pl.when(p == 0)
    def _():
        sb_ref[nc - 1 - c] = s_ref[...]
        kw = (kf * jnp.exp(pos * lgb)).astype(BF16)
        s_ref[...] = s_ref[...] * decay(lgb) + _dot_tn(kw, v)

    @pl.when(p == 1)
    def _():
        q = q_ref[...]
        qf = q.astype(F32)
        ii = lax.broadcasted_iota(I32, (cs, cs), 0)
        jj = lax.broadcasted_iota(I32, (cs, cs), 1)
        dd = ii - jj
        intra = jnp.exp(jnp.abs(dd).astype(F32) * jnp.where(dd >= 0, lgf, lgb))
        s = _dot_nt(q, k) * intra
        o = _dot(s.astype(BF16), v)
        qwf = (qf * jnp.exp((pos + 1.0) * lgf)).astype(BF16)
        qwb = (qf * jnp.exp((cs - pos) * lgb)).astype(BF16)
        o = o + _dot(qwf, s_ref[...].astype(BF16)) + _dot(qwb, sb_ref[c].astype(BF16))
        kw = (kf * jnp.exp((cs - 1.0 - pos) * lgf)).astype(BF16)
        s_ref[...] = s_ref[...] * decay(lgf) + _dot_tn(kw, v)
        o = o * lax.rsqrt(jnp.mean(o * o, axis=-1, keepdims=True) + EPS)
        g = g_ref[...].astype(F32)
        o_ref[...] = (o * gn_ref[...] * (g * jax.nn.sigmoid(g))).astype(BF16)


def _retention(lg, q, k, v, g, gn, B, T, cs=256):
    n = q.shape[0]
    nc = T // cs

    def chunk(p, c):
        return p * c + (1 - p) * (nc - 1 - c)

    blk = lambda b, h, p, c, lg: (b * nc + chunk(p, c), h)
    return pl.pallas_call(
        functools.partial(_ret_kernel, cs=cs, nc=nc),
        grid_spec=pltpu.PrefetchScalarGridSpec(
            num_scalar_prefetch=1,
            grid=(B, RET_HEADS, 2, nc),
            in_specs=[pl.BlockSpec((cs, LANES), blk)] * 4
            + [pl.BlockSpec((1, LANES), lambda b, h, p, c, lg: (0, h))],
            out_specs=pl.BlockSpec((cs, LANES), lambda b, h, p, c, lg: (b * nc + p * c, h)),
            scratch_shapes=[pltpu.VMEM((LANES, LANES), F32), pltpu.VMEM((nc, LANES, LANES), F32)],
        ),
        out_shape=jax.ShapeDtypeStruct((n, RET_HEADS * LANES), BF16),
        compiler_params=_params(("arbitrary",) * 4),
    )(lg, q, k, v, g, gn)


def _mla_kernel(q_ref, k_ref, v_ref, o_ref, *, tk, nk):
    q = q_ref[...]
    tq = q.shape[0]

    def body(j, carry):
        m, l, acc = carry
        off = pl.multiple_of(j * tk, tk)
        s = _dot_nt(q, k_ref[pl.ds(off, tk), :])
        m_new = jnp.maximum(m, jnp.max(s, axis=-1, keepdims=True))
        alpha = jnp.exp2(m - m_new)
        p = jnp.exp2(s - m_new)
        l = alpha * l + jnp.sum(p, axis=-1, keepdims=True)
        acc = alpha * acc + _dot(p.astype(BF16), v_ref[pl.ds(off, tk), :])
        return m_new, l, acc

    m0 = jnp.full((tq, 1), -jnp.inf, F32)
    l0 = jnp.zeros((tq, 1), F32)
    a0 = jnp.zeros((tq, LANES), F32)
    _, l, acc = lax.fori_loop(0, nk, body, (m0, l0, a0))
    o_ref[...] = (acc / l).astype(BF16)


def _mla(q, k, v, B, T, tq=256, tk=512):
    n = q.shape[0]
    tq = min(tq, T)
    tk = min(tk, T)
    nq = T // tq
    return pl.pallas_call(
        functools.partial(_mla_kernel, tk=tk, nk=T // tk),
        grid=(B, MLA_HEADS, nq),
        in_specs=[
            pl.BlockSpec((tq, LANES), lambda b, h, i: (b * nq + i, h)),
            pl.BlockSpec((T, LANES), lambda b, h, i: (b, h)),
            pl.BlockSpec((T, LANES), lambda b, h, i: (b, h)),
        ],
        out_specs=pl.BlockSpec((tq, LANES), lambda b, h, i: (b * nq + i, h)),
        out_shape=jax.ShapeDtypeStruct((n, MLA_HEADS * LANES), BF16),
        compiler_params=_params(("arbitrary",) * 3),
    )(q, k, v)


def _outproj_kernel(*refs, npairs):
    x_ref = refs[0]
    ys = refs[1:1 + npairs]
    ws = refs[1 + npairs:1 + 2 * npairs]
    g_ref, rt_ref, xo_ref, h_ref, aff_ref = refs[1 + 2 * npairs:]
    acc = x_ref[...]
    for y, w in zip(ys, ws):
        acc = acc + _dot(y[...], w[...])
    xo_ref[...] = acc
    h = _rms(acc, g_ref[...])
    h_ref[...] = h.astype(BF16)
    logits = lax.dot_general(rt_ref[...], h, (((1,), (1,)), ((), ())),
                             precision=lax.Precision.HIGHEST, preferred_element_type=F32)
    e = jnp.exp(logits - jnp.max(logits, axis=0, keepdims=True))
    aff_ref[...] = e / jnp.sum(e, axis=0, keepdims=True)


def _outproj_router(x, ys, ws, g, router_t, tm=256):
    n, d = x.shape
    row = lambda i: (i, 0)
    const = lambda i: (0, 0)
    return pl.pallas_call(
        functools.partial(_outproj_kernel, npairs=len(ys)),
        grid=(n // tm,),
        in_specs=[pl.BlockSpec((tm, d), row)]
        + [pl.BlockSpec((tm, y.shape[1]), row) for y in ys]
        + [pl.BlockSpec(w.shape, const) for w in ws]
        + [pl.BlockSpec((1, d), const), pl.BlockSpec((N_EXPERTS, d), const)],
        out_specs=[pl.BlockSpec((tm, d), row), pl.BlockSpec((tm, d), row),
                   pl.BlockSpec((N_EXPERTS, tm), lambda i: (0, i))],
        out_shape=[jax.ShapeDtypeStruct((n, d), F32), jax.ShapeDtypeStruct((n, d), BF16),
                   jax.ShapeDtypeStruct((N_EXPERTS, n), F32)],
        compiler_params=_params(("arbitrary",)),
    )(x, *ys, *ws, g, router_t)


def _rg_in_kernel(x_ref, g_ref, w_ref, gate_out, xr_out):
    hb = _rms(x_ref[...], g_ref[...]).astype(BF16)
    gate_out[...] = _dot(hb, w_ref[:, 0:D_RNN]).astype(BF16)
    xr_out[...] = _dot(hb, w_ref[:, D_RNN:2 * D_RNN])


def _rg_in(x, g, w, tm=256):
    n, d = x.shape
    row = lambda i: (i, 0)
    const = lambda i: (0, 0)
    return pl.pallas_call(
        _rg_in_kernel,
        grid=(n // tm,),
        in_specs=[pl.BlockSpec((tm, d), row), pl.BlockSpec((1, d), const),
                  pl.BlockSpec((d, 2 * D_RNN), const)],
        out_specs=[pl.BlockSpec((tm, D_RNN), row), pl.BlockSpec((tm, D_RNN), row)],
        out_shape=[jax.ShapeDtypeStruct((n, D_RNN), BF16), jax.ShapeDtypeStruct((n, D_RNN), F32)],
        compiler_params=_params(("arbitrary",)),
    )(x, g, w)


HALO = 8


def _rg_scan_kernel(xr_ref, prev_ref, next_ref, gate_ref, cw_ref, cb_ref, wa_ref, ba_ref, wx_ref,
                    bx_ref, lam_ref, y_ref, carry_ref, hb_ref, *, tc, nt):
    p = pl.program_id(2)
    c = pl.program_id(3)
    chunk = p * c + (1 - p) * (nt - 1 - c)

    @pl.when(c == 0)
    def _():
        carry_ref[...] = jnp.zeros_like(carry_ref)

    cur = xr_ref[...]
    prev = jnp.where(chunk > 0, prev_ref[...], 0.0)
    nxt = jnp.where(chunk < nt - 1, next_ref[...], 0.0)
    ext = jnp.concatenate([prev, cur, nxt], axis=0)
    xc = cb_ref[...]
    for j in range(CONV_W):
        o = HALO - CONV_LEFT + j
        xc = xc + ext[o:o + tc, :] * cw_ref[j:j + 1, :]
    xcb = xc.astype(BF16)
    r = jax.nn.sigmoid(_dot(xcb, wa_ref[0, 0]) + ba_ref[0])
    gi = jax.nn.sigmoid(_dot(xcb, wx_ref[0, 0]) + bx_ref[0])
    nl = -lam_ref[0]
    softplus = jnp.maximum(nl, 0.0) + jnp.log1p(jnp.exp(-jnp.abs(nl)))
    log_a = -RG_C * r * softplus
    a = jnp.exp(log_a)
    b = jnp.sqrt(1.0 - a * a) * (gi * xc)
    row = lax.broadcasted_iota(I32, (tc, LANES), 0)

    def scan(a, b, reverse):
        s = 1
        while s < tc:
            if reverse:
                keep = row < tc - s
                sh = tc - s
            else:
                keep = row >= s
                sh = s
            a_sh = jnp.where(keep, pltpu.roll(a, sh, 0), 1.0)
            b_sh = jnp.where(keep, pltpu.roll(b, sh, 0), 0.0)
            b = a * b_sh + b
            a = a * a_sh
            s *= 2
        return a * carry_ref[0:1, :] + b

    @pl.when(p == 0)
    def _():
        hs = scan(a, b, True)
        hb_ref[chunk] = hs
        carry_ref[...] = jnp.broadcast_to(hs[0:1, :], carry_ref.shape)

    @pl.when(p == 1)
    def _():
        hs = scan(a, b, False)
        carry_ref[...] = jnp.broadcast_to(hs[tc - 1:tc, :], carry_ref.shape)
        y_ref[...] = (jax.nn.gelu(gate_ref[...].astype(F32)) * (hs + hb_ref[chunk])).astype(BF16)


def _rg_scan(xr, gate, cw, cb, wa, ba, wx, bx, lam, B, T, tc=256):
    n = xr.shape[0]
    tc = min(tc, T)
    nt = T // tc
    hpc = tc // HALO

    def chunk(p, c):
        return p * c + (1 - p) * (nt - 1 - c)

    cur = lambda b, j, p, c: (b * nt + chunk(p, c), j)
    prv = lambda b, j, p, c: (jnp.maximum((b * nt + chunk(p, c)) * hpc - 1, 0), j)
    nxt = lambda b, j, p, c: (jnp.minimum((b * nt + chunk(p, c) + 1) * hpc, n // HALO - 1), j)
    par = lambda b, j, p, c: (1 - p, j, 0, 0)
    vec = lambda b, j, p, c: (1 - p, 0, j)
    return pl.pallas_call(
        functools.partial(_rg_scan_kernel, tc=tc, nt=nt),
        grid=(B, RG_BLOCKS, 2, nt),
        in_specs=[
            pl.BlockSpec((tc, LANES), cur),
            pl.BlockSpec((HALO, LANES), prv),
            pl.BlockSpec((HALO, LANES), nxt),
            pl.BlockSpec((tc, LANES), cur),
            pl.BlockSpec((CONV_W, LANES), lambda b, j, p, c: (0, j)),
            pl.BlockSpec((1, LANES), lambda b, j, p, c: (0, j)),
            pl.BlockSpec((1, 1, RG_BLK, RG_BLK), par),
            pl.BlockSpec((1, 1, LANES), vec),
            pl.BlockSpec((1, 1, RG_BLK, RG_BLK), par),
            pl.BlockSpec((1, 1, LANES), vec),
            pl.BlockSpec((1, 1, LANES), vec),
        ],
        out_specs=pl.BlockSpec((tc, LANES), lambda b, j, p, c: (b * nt + p * c, j)),
        out_shape=jax.ShapeDtypeStruct((n, D_RNN), BF16),
        scratch_shapes=[pltpu.VMEM((HALO, LANES), F32), pltpu.VMEM((nt, tc, LANES), F32)],
        compiler_params=_params(("arbitrary",) * 4),
    )(xr, xr, xr, gate, cw, cb, wa, ba, wx, bx, lam)


SEL_BLK = 256


def _select_kernel(aff_ref, pos_ref, gate_ref, cs_ref, *, cap, nblk):
    keys = pltpu.bitcast(aff_ref[...], I32)

    def search(i, thr):
        cand = thr | lax.shift_left(jnp.int32(1), 30 - i)
        cnt = jnp.sum(jnp.where(keys >= cand, 1.0, 0.0), axis=1, keepdims=True)
        return jnp.where(cnt >= cap, cand, thr)

    thr = lax.fori_loop(0, 31, search, jnp.zeros((N_EXPERTS, 1), I32))
    need = cap - jnp.sum(jnp.where(keys > thr, 1.0, 0.0), axis=1, keepdims=True)
    ri = lax.broadcasted_iota(I32, (SEL_BLK, SEL_BLK), 0)
    ci = lax.broadcasted_iota(I32, (SEL_BLK, SEL_BLK), 1)
    tri = jnp.where(ri <= ci, 1.0, 0.0).astype(BF16)

    def body(j, carry):
        ceq, csel = carry
        off = pl.multiple_of(j * SEL_BLK, SEL_BLK)
        a = aff_ref[:, pl.ds(off, SEL_BLK)]
        kk = pltpu.bitcast(a, I32)
        gt = kk > thr
        eq = kk == thr
        eqf = jnp.where(eq, 1.0, 0.0)
        eqc = _dot(eqf.astype(BF16), tri) + ceq
        sel = gt | (eq & (eqc <= need))
        self_ = jnp.where(sel, 1.0, 0.0)
        selc = _dot(self_.astype(BF16), tri) + csel
        pos_ref[:, pl.ds(off, SEL_BLK)] = jnp.where(sel, selc - 1.0, -1.0).astype(I32)
        gate_ref[:, pl.ds(off, SEL_BLK)] = jnp.where(sel, a, 0.0)
        cs_ref[j] = jnp.broadcast_to(csel, (N_EXPERTS, LANES)).astype(I32)
        return (ceq + jnp.sum(eqf, axis=1, keepdims=True),
                csel + jnp.sum(self_, axis=1, keepdims=True))

    z = jnp.zeros((N_EXPERTS, 1), F32)
    lax.fori_loop(0, nblk, body, (z, z))


def _select(aff_t, cap):
    e, n = aff_t.shape
    nblk = n // SEL_BLK
    return pl.pallas_call(
        functools.partial(_select_kernel, cap=cap, nblk=nblk),
        out_shape=[jax.ShapeDtypeStruct((e, n), I32), jax.ShapeDtypeStruct((e, n), F32),
                   jax.ShapeDtypeStruct((nblk, e, LANES), I32)],
        compiler_params=pltpu.CompilerParams(vmem_limit_bytes=VMEM_LIMIT),
    )(aff_t)


def _window(cs_ref, e, i):
    c = cs_ref[e, i]
    c1 = cs_ref[e, i + 1]
    shift = SLOT_ALIGN.bit_length() - 1
    b0 = lax.shift_left(lax.shift_right_logical(c, shift), shift)
    npass = lax.shift_right_logical(c1 - b0 + (SLOT_WIN - 1), SLOT_WIN.bit_length() - 1)
    return b0, npass


def _compact_kernel(cs_ref, h_ref, pos_ref, xg_ref, *, cap):
    e = pl.program_id(0)
    i = pl.program_id(1)

    @pl.when(i == 0)
    def _():
        xg_ref[...] = jnp.zeros_like(xg_ref)

    b0, npass = _window(cs_ref, e, i)
    pos = pos_ref[0]
    x = h_ref[...]
    rows = lax.broadcasted_iota(I32, (SLOT_WIN, SEL_BLK), 0)

    def body(k, _):
        lo = b0 + k * SLOT_WIN
        base = pl.multiple_of(jnp.minimum(lo, cap - SLOT_WIN), SLOT_ALIGN)
        tgt = jnp.where((pos >= lo) & (pos < lo + SLOT_WIN), pos - base, -1)
        onehot = jnp.where(rows == tgt, 1.0, 0.0).astype(BF16)
        xg_ref[0, pl.ds(base, SLOT_WIN), :] += _dot(onehot, x).astype(BF16)
        return 0

    lax.fori_loop(0, npass, body, 0)


def _compact(cs, h, pos3, cap):
    n, d = h.shape
    nt = n // SEL_BLK
    return pl.pallas_call(
        functools.partial(_compact_kernel, cap=cap),
        grid_spec=pltpu.PrefetchScalarGridSpec(
            num_scalar_prefetch=1,
            grid=(N_EXPERTS, nt),
            in_specs=[pl.BlockSpec((SEL_BLK, d), lambda e, i, cs: (i, 0)),
                      pl.BlockSpec((1, 1, SEL_BLK), lambda e, i, cs: (e, 0, i))],
            out_specs=pl.BlockSpec((1, cap, d), lambda e, i, cs: (e, 0, 0)),
        ),
        out_shape=jax.ShapeDtypeStruct((N_EXPERTS, cap, d), BF16),
        compiler_params=_params(("arbitrary",) * 2),
    )(cs, h, pos3)


def _ffn_kernel(x_ref, wg_ref, wu_ref, wd_ref, o_ref, acc_ref):
    f = pl.program_id(2)

    @pl.when(f == 0)
    def _():
        acc_ref[...] = jnp.zeros_like(acc_ref)

    x = x_ref[0]
    a = _dot(x, wg_ref[0])
    hid = (a * jax.nn.sigmoid(a)) * _dot(x, wu_ref[0])
    acc_ref[...] += _dot(hid.astype(BF16), wd_ref[0])

    @pl.when(f == pl.num_programs(2) - 1)
    def _():
        o_ref[0] = acc_ref[...].astype(BF16)


def _ffn(xg, wg, wu, wd, tm=1024, tf=512):
    e, cap, d = xg.shape
    fdim = wg.shape[2]
    tm = min(tm, cap)
    return pl.pallas_call(
        _ffn_kernel,
        grid=(e, cap // tm, fdim // tf),
        in_specs=[pl.BlockSpec((1, tm, d), lambda e, m, f: (e, m, 0)),
                  pl.BlockSpec((1, d, tf), lambda e, m, f: (e, 0, f)),
                  pl.BlockSpec((1, d, tf), lambda e, m, f: (e, 0, f)),
                  pl.BlockSpec((1, tf, d), lambda e, m, f: (e, f, 0))],
        out_specs=pl.BlockSpec((1, tm, d), lambda e, m, f: (e, m, 0)),
        out_shape=jax.ShapeDtypeStruct((e, cap, d), BF16),
        scratch_shapes=[pltpu.VMEM((tm, d), F32)],
        compiler_params=_params(("arbitrary",) * 3),
    )(xg, wg, wu, wd)


def _combine_kernel(cs_ref, x_ref, gate_ref, pos_ref, om_ref, xo_ref, buf, xbuf, sem, xsem,
                    *, cap, nt):
    i = pl.program_id(0)

    def win_copy(e, t, slot):
        b0, _ = _window(cs_ref, e, t)
        base = pl.multiple_of(jnp.minimum(b0, cap - SLOT_WIN), SLOT_ALIGN)
        return pltpu.make_async_copy(om_ref.at[e, pl.ds(base, SLOT_WIN), :], buf.at[slot, e],
                                     sem.at[slot, e])

    @pl.when(i == 0)
    def _():
        for e in range(N_EXPERTS):
            win_copy(e, 0, 0).start()

    @pl.when(i + 1 < nt)
    def _():
        for e in range(N_EXPERTS):
            win_copy(e, i + 1, (i + 1) % 2).start()

    slot = i % 2
    xo_ref[...] = x_ref[...]
    lane = lax.broadcasted_iota(I32, (SEL_BLK, SLOT_WIN), 1)

    def expand(pcol, lo, rows):
        base = jnp.minimum(lo, cap - SLOT_WIN)
        tgt = jnp.where((pcol >= lo) & (pcol < lo + SLOT_WIN), pcol - base, -1)
        return _dot(jnp.where(lane == tgt, 1.0, 0.0).astype(BF16), rows)

    for e in range(N_EXPERTS):
        win_copy(e, i, slot).wait()
        b0, npass = _window(cs_ref, e, i)
        pcol = pos_ref[:, e:e + 1]
        gcol = gate_ref[:, e:e + 1]
        xo_ref[...] += gcol * expand(pcol, b0, buf[slot, e])

        def extra(k, _, e=e, b0=b0, pcol=pcol, gcol=gcol):
            lo = b0 + k * SLOT_WIN
            base = pl.multiple_of(jnp.minimum(lo, cap - SLOT_WIN), SLOT_ALIGN)
            cp = pltpu.make_async_copy(om_ref.at[e, pl.ds(base, SLOT_WIN), :], xbuf, xsem)
            cp.start()
            cp.wait()
            xo_ref[...] += gcol * expand(pcol, lo, xbuf[...])
            return 0

        lax.fori_loop(1, npass, extra, 0)


def _combine(cs, x, gate_t, pos_t, om, cap):
    n, d = x.shape
    nt = n // SEL_BLK
    row = lambda i, cs: (i, 0)
    return pl.pallas_call(
        functools.partial(_combine_kernel, cap=cap, nt=nt),
        grid_spec=pltpu.PrefetchScalarGridSpec(
            num_scalar_prefetch=1,
            grid=(nt,),
            in_specs=[pl.BlockSpec((SEL_BLK, d), row),
                      pl.BlockSpec((SEL_BLK, N_EXPERTS), row),
                      pl.BlockSpec((SEL_BLK, N_EXPERTS), row),
                      pl.BlockSpec(memory_space=pl.ANY)],
            out_specs=pl.BlockSpec((SEL_BLK, d), row),
            scratch_shapes=[pltpu.VMEM((2, N_EXPERTS, SLOT_WIN, d), BF16),
                            pltpu.VMEM((SLOT_WIN, d), BF16),
                            pltpu.SemaphoreType.DMA((2, N_EXPERTS)),
                            pltpu.SemaphoreType.DMA(())],
        ),
        out_shape=jax.ShapeDtypeStruct((n, d), F32),
        compiler_params=_params(("arbitrary",)),
    )(cs, x, gate_t, pos_t, om)


def _moe(x, h, aff_t, wg, wu, wd):
    n = x.shape[0]
    cap = max(1, EC_FACTOR * n // N_EXPERTS)
    pos, gate, cs3 = _select(aff_t, cap)
    cs = jnp.concatenate([cs3[:, :, 0].T, jnp.full((N_EXPERTS, 1), cap, I32)], axis=1)
    xg = _compact(cs, h, pos.reshape(N_EXPERTS, 1, n), cap)
    om = _ffn(xg, wg, wu, wd)
    return _combine(cs, x, gate.T, pos.T, om, cap)


def _final_kernel(x_ref, g_ref, o_ref):
    o_ref[...] = _rms(x_ref[...], g_ref[...])


def _final_norm(x, g, tm=512):
    n, d = x.shape
    return pl.pallas_call(
        _final_kernel,
        grid=(n // tm,),
        in_specs=[pl.BlockSpec((tm, d), lambda i: (i, 0)), pl.BlockSpec((1, d), lambda i: (0, 0))],
        out_specs=pl.BlockSpec((tm, d), lambda i: (i, 0)),
        out_shape=jax.ShapeDtypeStruct((n, d), F32),
        compiler_params=_params(("arbitrary",)),
    )(x, g)


def _rot_half_cols(w):
    half = w.shape[-1] // 2
    return jnp.concatenate([-w[..., half:], w[..., :half]], axis=-1)


def _pad_heads(w, heads, dim, lead=0):
    d = w.shape[0]
    w = w.reshape(d, heads, dim)
    w = jnp.pad(w, ((0, 0), (0, 0), (lead, LANES - lead - dim)))
    return w.reshape(d, heads * LANES)


def _prep_ab(w_in, w_uq, w_ukv, w_out):
    d = w_in.shape[0]
    cuts = [0, 256, 512, 1024, 1536, 1792, 1920, 1952]
    q_r, k_r, v_r, g_r, c_q, c_kv, k_pe = [w_in[:, a:b] for a, b in zip(cuts[:-1], cuts[1:])]
    q4 = q_r.reshape(d, RET_HEADS, RET_DK)
    k4 = k_r.reshape(d, RET_HEADS, RET_DK) * (RET_DK ** -0.5)
    ph = lambda w: _pad_heads(w.reshape(d, -1), RET_HEADS, RET_DK)
    kpe = jnp.pad(k_pe, ((0, 0), (MLA_NOPE, LANES - MLA_NOPE - MLA_ROPE)))
    kpe_rot = jnp.pad(_rot_half_cols(k_pe), ((0, 0), (MLA_NOPE, LANES - MLA_NOPE - MLA_ROPE)))
    w_ext = jnp.concatenate([ph(q4), ph(_rot_half_cols(q4)), ph(k4), ph(_rot_half_cols(k4)),
                             v_r, g_r, c_q, c_kv, kpe, kpe_rot], axis=1).astype(BF16)
    uq = w_uq.reshape(MLA_Q_LORA, MLA_HEADS, MLA_NOPE + MLA_ROPE)
    qa = jnp.pad(uq, ((0, 0), (0, 0), (0, LANES - MLA_NOPE - MLA_ROPE)))
    qb = jnp.pad(_rot_half_cols(uq[..., MLA_NOPE:]),
                 ((0, 0), (0, 0), (MLA_NOPE, LANES - MLA_NOPE - MLA_ROPE)))
    wuq = jnp.concatenate([qa.reshape(MLA_Q_LORA, -1), qb.reshape(MLA_Q_LORA, -1)], axis=1)
    ukv = w_ukv.reshape(MLA_KV_LORA, MLA_HEADS, MLA_NOPE + MLA_V)
    kn = jnp.pad(ukv[..., :MLA_NOPE], ((0, 0), (0, 0), (0, LANES - MLA_NOPE)))
    vv = jnp.pad(ukv[..., MLA_NOPE:], ((0, 0), (0, 0), (0, LANES - MLA_V)))
    wukv = jnp.concatenate([kn.reshape(MLA_KV_LORA, -1), vv.reshape(MLA_KV_LORA, -1)], axis=1)
    wo_a = w_out[:RET_HEADS * RET_DV]
    wo_b = w_out[RET_HEADS * RET_DV:].reshape(MLA_HEADS, MLA_V, d)
    wo_b = jnp.pad(wo_b, ((0, 0), (0, LANES - MLA_V), (0, 0))).reshape(MLA_HEADS * LANES, d)
    return w_ext, wuq.astype(BF16), wukv.astype(BF16), wo_a.astype(BF16), wo_b.astype(BF16)


def _rope_tabs(T):
    def tables(dim):
        inv = 1.0 / (ROPE_BASE ** (jnp.arange(0, dim, 2, dtype=F32) / dim))
        ang = jnp.arange(T, dtype=F32)[:, None] * inv[None, :]
        return jnp.cos(ang), jnp.sin(ang)

    cr, sr = tables(RET_DK)
    cr = jnp.tile(cr, (1, LANES // (RET_DK // 2)))
    sr = jnp.tile(sr, (1, LANES // (RET_DK // 2)))
    cm, sm = tables(MLA_ROPE)
    scale = (MLA_NOPE + MLA_ROPE) ** -0.5 * math.log2(math.e)
    ones = jnp.ones((T, MLA_NOPE), F32)
    z_lo = jnp.zeros((T, MLA_NOPE), F32)
    z_hi = jnp.zeros((T, LANES - MLA_NOPE - MLA_ROPE), F32)
    cq = jnp.concatenate([ones, cm, cm, z_hi], axis=1) * scale
    sq = jnp.concatenate([z_lo, sm, sm, z_hi], axis=1) * scale
    ck = jnp.concatenate([z_lo, cm, cm, z_hi], axis=1)
    sk = jnp.concatenate([z_lo, sm, sm, z_hi], axis=1)
    return cr, sr, cq, sq, ck, sk


def _trunk(x3, p):
    B, T, d = x3.shape
    x = x3.reshape(B * T, d)
    tabs = _rope_tabs(T)
    for l in range(DEPTH):
        j = l // 2
        g_mix = p["norm_mix"][l].reshape(1, d)
        g_ffn = p["norm_ffn"][l].reshape(1, d)
        router_t = p["moe_router"][l].T
        if l % 2 == 0:
            w_ext, wuq, wukv, wo_a, wo_b = p["ab"][j]
            qr, kr, vr, gr, q, k, v = _ab_in(x, g_mix, w_ext, wuq, wukv,
                                             p["mla_q_norm"][j].reshape(1, -1),
                                             p["mla_kv_norm"][j].reshape(1, -1), tabs, T)
            lg = jax.nn.log_sigmoid(p["ret_decay_logit"][j].astype(F32))
            y_a = _retention(lg, qr, kr, vr, gr, p["ret_gn"][j].reshape(1, -1), B, T)
            y_b = _mla(q, k, v, B, T)
            x, h, aff_t = _outproj_router(x, [y_a, y_b], [wo_a, wo_b], g_ffn, router_t)
        else:
            gate, xr = _rg_in(x, g_mix, p["rg_w_in"][j])
            y = _rg_scan(xr, gate, p["rg_conv_w"][j], p["rg_conv_b"][j].reshape(1, -1),
                         p["rg_wa"][j], p["rg_ba"][j].reshape(2, 1, -1), p["rg_wx"][j],
                         p["rg_bx"][j].reshape(2, 1, -1), p["rg_lambda"][j].reshape(2, 1, -1), B, T)
            x, h, aff_t = _outproj_router(x, [y], [p["rg_w_out"][j]], g_ffn, router_t)
        x = _moe(x, h, aff_t, p["moe_w_gate"][l], p["moe_w_up"][l], p["moe_w_down"][l])
    return _final_norm(x, p["norm_final"].reshape(1, d)).reshape(B, T, d)


def kernel(x_prompt, x_sample, norm_mix, norm_ffn, norm_final, ab_w_in, ret_decay_logit, ret_gn, mla_q_norm, mla_w_uq, mla_kv_norm, mla_w_ukv, ab_w_out, rg_w_in, rg_conv_w, rg_conv_b, rg_wa, rg_ba, rg_wx, rg_bx, rg_lambda, rg_w_out, moe_router, moe_w_gate, moe_w_up, moe_w_down):
    p = dict(
        norm_mix=norm_mix, norm_ffn=norm_ffn, norm_final=norm_final,
        ret_decay_logit=ret_decay_logit, ret_gn=ret_gn, mla_q_norm=mla_q_norm,
        mla_kv_norm=mla_kv_norm, rg_conv_w=rg_conv_w, rg_conv_b=rg_conv_b, rg_ba=rg_ba,
        rg_bx=rg_bx, rg_lambda=rg_lambda, moe_router=moe_router,
        ab=[_prep_ab(ab_w_in[j], mla_w_uq[j], mla_w_ukv[j], ab_w_out[j])
            for j in range(ab_w_in.shape[0])],
        rg_w_in=rg_w_in.astype(BF16), rg_wa=rg_wa.astype(BF16), rg_wx=rg_wx.astype(BF16),
        rg_w_out=rg_w_out.astype(BF16), moe_w_gate=moe_w_gate.astype(BF16),
        moe_w_up=moe_w_up.astype(BF16), moe_w_down=moe_w_down.astype(BF16),
    )
    return _trunk(x_prompt, p), _trunk(x_sample, p)
```

```python
import functools
import math

import jax
import jax.numpy as jnp
from jax import lax
from jax.experimental import pallas as pl
from jax.experimental.pallas import tpu as pltpu

F32 = jnp.float32
BF16 = jnp.bfloat16
I32 = jnp.int32

D_MODEL = 1024
DEPTH = 4
EPS = 1e-6
ROPE_BASE = 10000.0
RET_HEADS = 4
RET_DK = 64
RET_DV = 128
MLA_HEADS = 8
MLA_NOPE = 64
MLA_ROPE = 32
MLA_V = 64
MLA_Q_LORA = 256
MLA_KV_LORA = 128
D_RNN = D_MODEL
RG_BLOCKS = 8
RG_BLK = D_RNN // RG_BLOCKS
CONV_W = 4
CONV_LEFT = CONV_W // 2
RG_C = 8.0
N_EXPERTS = 16
EC_FACTOR = 2
D_EXPERT = 2 * D_MODEL

LANES = 128
SLOT_ALIGN = 64
SLOT_WIN = 128
VMEM_LIMIT = 56 * 1024 * 1024


def _params(sem):
    return pltpu.CompilerParams(dimension_semantics=sem, vmem_limit_bytes=VMEM_LIMIT)


def _rms(x, g):
    return x * lax.rsqrt(jnp.mean(x * x, axis=-1, keepdims=True) + EPS) * g


def _dot(a, b):
    return jnp.dot(a, b, preferred_element_type=F32)


def _dot_nt(a, b):
    return lax.dot_general(a, b, (((1,), (1,)), ((), ())), preferred_element_type=F32)


def _dot_tn(a, b):
    return lax.dot_general(a, b, (((0,), (0,)), ((), ())), preferred_element_type=F32)


AB_SEG = {
    "q": (0, 512), "q_rot": (512, 1024), "k": (1024, 1536), "k_rot": (1536, 2048),
    "v": (2048, 2560), "g": (2560, 3072), "c_q": (3072, 3328), "c_kv": (3328, 3456),
    "k_pe": (3456, 3584), "k_pe_rot": (3584, 3712),
}
AB_EXT = 3712


def _ab_in_kernel(x_ref, g_ref, w_ref, wuq_ref, wukv_ref, qn_ref, kvn_ref, cr_ref, sr_ref,
                  cq_ref, sq_ref, ck_ref, sk_ref,
                  qr_out, kr_out, vr_out, gr_out, q_out, k_out, v_out):
    hb = _rms(x_ref[...], g_ref[...]).astype(BF16)

    def seg(name):
        a, b = AB_SEG[name]
        return _dot(hb, w_ref[:, a:b])

    cr = jnp.tile(cr_ref[...], (1, RET_HEADS))
    sr = jnp.tile(sr_ref[...], (1, RET_HEADS))
    qr_out[...] = (seg("q") * cr + seg("q_rot") * sr).astype(BF16)
    kr_out[...] = (seg("k") * cr + seg("k_rot") * sr).astype(BF16)
    vr_out[...] = seg("v").astype(BF16)
    gr_out[...] = seg("g").astype(BF16)

    cqn = _rms(seg("c_q"), qn_ref[...]).astype(BF16)
    hq = MLA_HEADS * LANES
    qa = _dot(cqn, wuq_ref[:, 0:hq])
    qb = _dot(cqn, wuq_ref[:, hq:2 * hq])
    q_out[...] = (qa * jnp.tile(cq_ref[...], (1, MLA_HEADS))
                  + qb * jnp.tile(sq_ref[...], (1, MLA_HEADS))).astype(BF16)

    ckvn = _rms(seg("c_kv"), kvn_ref[...]).astype(BF16)
    kn = _dot(ckvn, wukv_ref[:, 0:hq])
    kpe = seg("k_pe") * ck_ref[...] + seg("k_pe_rot") * sk_ref[...]
    k_out[...] = (kn + jnp.tile(kpe, (1, MLA_HEADS))).astype(BF16)
    vv = _dot(ckvn, wukv_ref[:, hq:2 * hq])
    lane = lax.broadcasted_iota(I32, vv.shape, 1)
    v_out[...] = jnp.where(lane % LANES == MLA_V, 1.0, vv).astype(BF16)


def _ab_in(x, g, w_ext, wuq, wukv, qn, kvn, tabs, T, tm=256):
    n, d = x.shape
    nt = T // tm
    row = lambda i: (i, 0)
    const = lambda i: (0, 0)
    tab = lambda i: (i % nt, 0)
    hq = MLA_HEADS * LANES
    hr = RET_HEADS * LANES
    outs = [jax.ShapeDtypeStruct((n, w), BF16) for w in (hr, hr, hr, hr, hq, hq, hq)]
    return pl.pallas_call(
        _ab_in_kernel,
        name="ab_in",
        grid=(n // tm,),
        in_specs=[
            pl.BlockSpec((tm, d), row),
            pl.BlockSpec((1, d), const),
            pl.BlockSpec((d, AB_EXT), const),
            pl.BlockSpec((MLA_Q_LORA, 2 * hq), const),
            pl.BlockSpec((MLA_KV_LORA, 2 * hq), const),
            pl.BlockSpec((1, MLA_Q_LORA), const),
            pl.BlockSpec((1, MLA_KV_LORA), const),
        ] + [pl.BlockSpec((tm, LANES), tab)] * 6,
        out_specs=[pl.BlockSpec((tm, w.shape[1]), row) for w in outs],
        out_shape=outs,
        compiler_params=_params(("arbitrary",)),
    )(x, g, w_ext, wuq, wukv, qn, kvn, *tabs)


def _ret_kernel(lg_ref, q_ref, k_ref, v_ref, g_ref, gn_ref, o_ref, s_ref, sb_ref, *, cs, nc):
    p = pl.program_id(1)
    c = pl.program_id(2)
    pos = lax.broadcasted_iota(I32, (cs, LANES), 0).astype(F32)

    @pl.when(c == 0)
    def _():
        s_ref[...] = jnp.zeros_like(s_ref)

    def decay(lg):
        return jnp.exp(jnp.full((1, LANES), cs, F32) * lg)

    def head(ref, h):
        return ref[:, h * LANES:(h + 1) * LANES]

    @pl.when(p == 0)
    def _():
        for h in range(RET_HEADS):
            lgb = lg_ref[1, h]
            sb_ref[nc - 1 - c, h] = s_ref[h]
            kw = (head(k_ref, h).astype(F32) * jnp.exp(pos * lgb)).astype(BF16)
            s_ref[h] = s_ref[h] * decay(lgb) + _dot_tn(kw, head(v_ref, h))

    @pl.when(p == 1)
    def _():
        ii = lax.broadcasted_iota(I32, (cs, cs), 0)
        jj = lax.broadcasted_iota(I32, (cs, cs), 1)
        dd = ii - jj
        dist = jnp.abs(dd).astype(F32)
        for h in range(RET_HEADS):
            lgf = lg_ref[0, h]
            lgb = lg_ref[1, h]
            q = head(q_ref, h)
            k = head(k_ref, h)
            v = head(v_ref, h)
            qf = q.astype(F32)
            intra = jnp.exp(dist * jnp.where(dd >= 0, lgf, lgb))
            o = _dot((_dot_nt(q, k) * intra).astype(BF16), v)
            qwf = (qf * jnp.exp((pos + 1.0) * lgf)).astype(BF16)
            qwb = (qf * jnp.exp((cs - pos) * lgb)).astype(BF16)
            o = o + _dot(qwf, s_ref[h].astype(BF16)) + _dot(qwb, sb_ref[c, h].astype(BF16))
            kw = (k.astype(F32) * jnp.exp((cs - 1.0 - pos) * lgf)).astype(BF16)
            s_ref[h] = s_ref[h] * decay(lgf) + _dot_tn(kw, v)
            o = o * lax.rsqrt(jnp.mean(o * o, axis=-1, keepdims=True) + EPS)
            g = head(g_ref, h).astype(F32)
            o_ref[:, h * LANES:(h + 1) * LANES] = (
                o * head(gn_ref, h) * (g * jax.nn.sigmoid(g))).astype(BF16)


def _retention(lg, q, k, v, g, gn, B, T, cs=256):
    n, w = q.shape
    nc = T // cs

    def chunk(p, c):
        return p * c + (1 - p) * (nc - 1 - c)

    blk = lambda b, p, c, lg: (b * nc + chunk(p, c), 0)
    return pl.pallas_call(
        functools.partial(_ret_kernel, cs=cs, nc=nc),
        name="retention",
        grid_spec=pltpu.PrefetchScalarGridSpec(
            num_scalar_prefetch=1,
            grid=(B, 2, nc),
            in_specs=[pl.BlockSpec((cs, w), blk)] * 4
            + [pl.BlockSpec((1, w), lambda b, p, c, lg: (0, 0))],
            out_specs=pl.BlockSpec((cs, w), lambda b, p, c, lg: (b * nc + p * c, 0)),
            scratch_shapes=[pltpu.VMEM((RET_HEADS, LANES, LANES), F32),
                            pltpu.VMEM((nc, RET_HEADS, LANES, LANES), F32)],
        ),
        out_shape=jax.ShapeDtypeStruct((n, w), BF16),
        compiler_params=_params(("arbitrary",) * 3),
    )(lg, q, k, v, g, gn)


def _mla_kernel(q_ref, k_ref, v_ref, o_ref, s_ref, p_ref, *, tk, nk, nchain):
    rows = q_ref.shape[0] // nchain
    qs = [q_ref[c * rows:(c + 1) * rows, :] for c in range(nchain)]

    def keys(j):
        return k_ref[pl.ds(pl.multiple_of(j * tk, tk), tk), :]

    def vals(j):
        return v_ref[pl.ds(pl.multiple_of(j * tk, tk), tk), :]

    def step(j, slot, carry):
        kn = keys(jnp.minimum(j + 1, nk - 1))
        vp = vals(jnp.maximum(j - 1, 0))
        out = []
        for c, (m, a_prev, acc) in enumerate(carry):
            s_ref[c, 1 - slot] = _dot_nt(qs[c], kn)
            acc = a_prev * acc + _dot(p_ref[c, 1 - slot], vp)
            s = s_ref[c, slot]
            m_new = jnp.maximum(m, jnp.max(s, axis=-1, keepdims=True))
            p_ref[c, slot] = jnp.exp2(s - m_new).astype(BF16)
            out.append((m_new, jnp.exp2(m - m_new), acc))
        return tuple(out)

    k0 = keys(0)
    for c in range(nchain):
        s_ref[c, 0] = _dot_nt(qs[c], k0)
        p_ref[c, 1] = jnp.zeros(p_ref.shape[2:], BF16)
    init = tuple((jnp.full((rows, 1), -jnp.inf, F32), jnp.ones((rows, 1), F32),
                  jnp.zeros((rows, LANES), F32)) for _ in range(nchain))
    res = lax.fori_loop(0, nk // 2, lambda t, cr: step(2 * t + 1, 1, step(2 * t, 0, cr)), init)
    vl = vals(nk - 1)
    for c, (_, a_prev, acc) in enumerate(res):
        acc = a_prev * acc + _dot(p_ref[c, 1], vl)
        o_ref[c * rows:(c + 1) * rows, :] = (acc / acc[:, MLA_V:MLA_V + 1]).astype(BF16)


def _mla(q, k, v, B, T, tq=512, tk=1024, nchain=2):
    n = q.shape[0]
    tq = min(tq, T)
    tk = min(tk, T // 2)
    nq = T // tq
    nk = T // tk
    assert nk % 2 == 0
    rows = tq // nchain
    return pl.pallas_call(
        functools.partial(_mla_kernel, tk=tk, nk=nk, nchain=nchain),
        name="mla",
        grid=(B, MLA_HEADS, nq),
        in_specs=[
            pl.BlockSpec((tq, LANES), lambda b, h, i: (b * nq + i, h)),
            pl.BlockSpec((T, LANES), lambda b, h, i: (b, h)),
            pl.BlockSpec((T, LANES), lambda b, h, i: (b, h)),
        ],
        out_specs=pl.BlockSpec((tq, LANES), lambda b, h, i: (b * nq + i, h)),
        out_shape=jax.ShapeDtypeStruct((n, MLA_HEADS * LANES), BF16),
        scratch_shapes=[pltpu.VMEM((nchain, 2, rows, tk), F32), pltpu.VMEM((nchain, 2, rows, tk), BF16)],
        compiler_params=_params(("arbitrary",) * 3),
    )(q, k, v)


def _outproj_kernel(*refs, npairs):
    x_ref = refs[0]
    ys = refs[1:1 + npairs]
    ws = refs[1 + npairs:1 + 2 * npairs]
    g_ref, rt_ref, xo_ref, h_ref, aff_ref = refs[1 + 2 * npairs:]
    half = x_ref.shape[0] // 2
    for r in (slice(0, half), slice(half, 2 * half)):
        acc = x_ref[r, :]
        for y, w in zip(ys, ws):
            acc = acc + _dot(y[r, :], w[...])
        xo_ref[r, :] = acc
        h = _rms(acc, g_ref[...])
        h_hi = h.astype(BF16)
        h_ref[r, :] = h_hi
        part = _dot_nt(rt_ref[...], h_hi)
        h_lo = (h - h_hi.astype(F32)).astype(BF16)
        logits = (part[0:N_EXPERTS] + part[N_EXPERTS:2 * N_EXPERTS]
                  + _dot_nt(rt_ref[0:N_EXPERTS, :], h_lo))
        e = jnp.exp(logits - jnp.max(logits, axis=0, keepdims=True))
        aff_ref[:, r] = e / jnp.sum(e, axis=0, keepdims=True)


def _outproj_router(x, ys, ws, g, router_t, tm=512):
    n, d = x.shape
    row = lambda i: (i, 0)
    const = lambda i: (0, 0)
    return pl.pallas_call(
        functools.partial(_outproj_kernel, npairs=len(ys)),
        name="outproj_router",
        grid=(n // tm,),
        in_specs=[pl.BlockSpec((tm, d), row)]
        + [pl.BlockSpec((tm, y.shape[1]), row) for y in ys]
        + [pl.BlockSpec(w.shape, const) for w in ws]
        + [pl.BlockSpec((1, d), const), pl.BlockSpec((2 * N_EXPERTS, d), const)],
        out_specs=[pl.BlockSpec((tm, d), row), pl.BlockSpec((tm, d), row),
                   pl.BlockSpec((N_EXPERTS, tm), lambda i: (0, i))],
        out_shape=[jax.ShapeDtypeStruct((n, d), F32), jax.ShapeDtypeStruct((n, d), BF16),
                   jax.ShapeDtypeStruct((N_EXPERTS, n), F32)],
        compiler_params=_params(("arbitrary",)),
    )(x, *ys, *ws, g, router_t)


def _rg_in_kernel(x_ref, g_ref, w_ref, gate_out, xr_out):
    hb = _rms(x_ref[...], g_ref[...]).astype(BF16)
    gate_out[...] = _dot(hb, w_ref[:, 0:D_RNN]).astype(BF16)
    xr_out[...] = _dot(hb, w_ref[:, D_RNN:2 * D_RNN])


def _rg_in(x, g, w, tm=256):
    n, d = x.shape
    row = lambda i: (i, 0)
    const = lambda i: (0, 0)
    return pl.pallas_call(
        _rg_in_kernel,
        name="rg_in",
        grid=(n // tm,),
        in_specs=[pl.BlockSpec((tm, d), row), pl.BlockSpec((1, d), const),
                  pl.BlockSpec((d, 2 * D_RNN), const)],
        out_specs=[pl.BlockSpec((tm, D_RNN), row), pl.BlockSpec((tm, D_RNN), row)],
        out_shape=[jax.ShapeDtypeStruct((n, D_RNN), BF16), jax.ShapeDtypeStruct((n, D_RNN), F32)],
        compiler_params=_params(("arbitrary",)),
    )(x, g, w)


HALO = 8
RG_LANES = 512
SUBLANES = 8


def _rg_scan_kernel(xr_ref, prev_ref, next_ref, gate_ref, cw_ref, cb_ref, wa_ref, ba_ref, wx_ref,
                    bx_ref, lam_ref, y_ref, carry_ref, hb_ref, *, tc, nt):
    p = pl.program_id(2)
    c = pl.program_id(3)
    chunk = p * c + (1 - p) * (nt - 1 - c)

    @pl.when(c == 0)
    def _():
        carry_ref[...] = jnp.zeros_like(carry_ref)

    cur = xr_ref[...]
    prev = jnp.where(chunk > 0, prev_ref[...], 0.0)
    nxt = jnp.where(chunk < nt - 1, next_ref[...], 0.0)
    ext = jnp.concatenate([prev, cur, nxt], axis=0)
    xc = cb_ref[...]
    for j in range(CONV_W):
        o = HALO - CONV_LEFT + j
        xc = xc + ext[o:o + tc, :] * cw_ref[j:j + 1, :]
    xcb = xc.astype(BF16)

    def block_diag(w_ref):
        return jnp.concatenate(
            [_dot(xcb[:, k * RG_BLK:(k + 1) * RG_BLK], w_ref[0, k])
             for k in range(RG_LANES // RG_BLK)], axis=1)

    r = jax.nn.sigmoid(block_diag(wa_ref) + ba_ref[0])
    gi = jax.nn.sigmoid(block_diag(wx_ref) + bx_ref[0])
    nl = -lam_ref[0]
    softplus = jnp.maximum(nl, 0.0) + jnp.log1p(jnp.exp(-jnp.abs(nl)))
    log_a = -RG_C * r * softplus
    a = jnp.exp(log_a)
    b = jnp.sqrt(1.0 - a * a) * (gi * xc)
    sub = lax.broadcasted_iota(I32, (tc, RG_LANES), 0) % SUBLANES

    def scan(a, b, reverse):
        for s in (1, 2, 4):
            keep = (sub < SUBLANES - s) if reverse else (sub >= s)
            sh = tc - s if reverse else s
            a_sh = jnp.where(keep, pltpu.roll(a, sh, 0), 1.0)
            b_sh = jnp.where(keep, pltpu.roll(b, sh, 0), 0.0)
            b = a * b_sh + b
            a = a * a_sh
        groups = tc // SUBLANES
        out = [None] * groups
        carry = carry_ref[0:1, :]
        for g in (range(groups - 1, -1, -1) if reverse else range(groups)):
            rows = slice(g * SUBLANES, (g + 1) * SUBLANES)
            out[g] = a[rows] * carry + b[rows]
            carry = out[g][0:1] if reverse else out[g][SUBLANES - 1:SUBLANES]
        carry_ref[...] = jnp.broadcast_to(carry, carry_ref.shape)
        return jnp.concatenate(out, axis=0)

    @pl.when(p == 0)
    def _():
        hb_ref[chunk] = scan(a, b, True).astype(BF16)

    @pl.when(p == 1)
    def _():
        hs = scan(a, b, False) + hb_ref[chunk].astype(F32)
        y_ref[...] = (jax.nn.gelu(gate_ref[...].astype(F32)) * hs).astype(BF16)


def _rg_scan(xr, gate, cw, cb, wa, ba, wx, bx, lam, B, T, tc=256):
    n = xr.shape[0]
    tc = min(tc, T)
    nt = T // tc
    hpc = tc // HALO
    gb = RG_LANES // RG_BLK

    def chunk(p, c):
        return p * c + (1 - p) * (nt - 1 - c)

    cur = lambda b, j, p, c: (b * nt + chunk(p, c), j)
    prv = lambda b, j, p, c: (jnp.maximum((b * nt + chunk(p, c)) * hpc - 1, 0), j)
    nxt = lambda b, j, p, c: (jnp.minimum((b * nt + chunk(p, c) + 1) * hpc, n // HALO - 1), j)
    par = lambda b, j, p, c: (1 - p, j, 0, 0)
    vec = lambda b, j, p, c: (1 - p, 0, j)
    return pl.pallas_call(
        functools.partial(_rg_scan_kernel, tc=tc, nt=nt),
        name="rg_scan",
        grid=(B, D_RNN // RG_LANES, 2, nt),
        in_specs=[
            pl.BlockSpec((tc, RG_LANES), cur),
            pl.BlockSpec((HALO, RG_LANES), prv),
            pl.BlockSpec((HALO, RG_LANES), nxt),
            pl.BlockSpec((tc, RG_LANES), cur),
            pl.BlockSpec((CONV_W, RG_LANES), lambda b, j, p, c: (0, j)),
            pl.BlockSpec((1, RG_LANES), lambda b, j, p, c: (0, j)),
            pl.BlockSpec((1, gb, RG_BLK, RG_BLK), par),
            pl.BlockSpec((1, 1, RG_LANES), vec),
            pl.BlockSpec((1, gb, RG_BLK, RG_BLK), par),
            pl.BlockSpec((1, 1, RG_LANES), vec),
            pl.BlockSpec((1, 1, RG_LANES), vec),
        ],
        out_specs=pl.BlockSpec((tc, RG_LANES), lambda b, j, p, c: (b * nt + p * c, j)),
        out_shape=jax.ShapeDtypeStruct((n, D_RNN), BF16),
        scratch_shapes=[pltpu.VMEM((HALO, RG_LANES), F32), pltpu.VMEM((nt, tc, RG_LANES), BF16)],
        compiler_params=_params(("arbitrary",) * 4),
    )(xr, xr, xr, gate, cw, cb, wa, ba, wx, bx, lam)


SEL_BLK = 256


def _select_kernel(aff_ref, pos_ref, gate_ref, cs_ref, *, cap, nblk):
    keys = pltpu.bitcast(aff_ref[...], I32)

    def search(i, thr):
        cand = thr | lax.shift_left(jnp.int32(1), 30 - i)
        cnt = jnp.sum(jnp.where(keys >= cand, 1.0, 0.0), axis=1, keepdims=True)
        return jnp.where(cnt >= cap, cand, thr)

    thr = lax.fori_loop(0, 31, search, jnp.zeros((N_EXPERTS, 1), I32))
    need = cap - jnp.sum(jnp.where(keys > thr, 1.0, 0.0), axis=1, keepdims=True)
    ri = lax.broadcasted_iota(I32, (SEL_BLK, SEL_BLK), 0)
    ci = lax.broadcasted_iota(I32, (SEL_BLK, SEL_BLK), 1)
    tri = jnp.where(ri <= ci, 1.0, 0.0).astype(BF16)

    def body(j, carry):
        ceq, csel = carry
        off = pl.multiple_of(j * SEL_BLK, SEL_BLK)
        a = aff_ref[:, pl.ds(off, SEL_BLK)]
        kk = pltpu.bitcast(a, I32)
        gt = kk > thr
        eq = kk == thr
        eqf = jnp.where(eq, 1.0, 0.0)
        eqc = _dot(eqf.astype(BF16), tri) + ceq
        sel = gt | (eq & (eqc <= need))
        self_ = jnp.where(sel, 1.0, 0.0)
        selc = _dot(self_.astype(BF16), tri) + csel
        pos_ref[:, pl.ds(off, SEL_BLK)] = jnp.where(sel, selc - 1.0, -1.0).astype(I32)
        gate_ref[:, pl.ds(off, SEL_BLK)] = jnp.where(sel, a, 0.0)
        cs_ref[j] = jnp.broadcast_to(csel, (N_EXPERTS, LANES)).astype(I32)
        return (ceq + jnp.sum(eqf, axis=1, keepdims=True),
                csel + jnp.sum(self_, axis=1, keepdims=True))

    z = jnp.zeros((N_EXPERTS, 1), F32)
    lax.fori_loop(0, nblk, body, (z, z))


def _select(aff_t, cap):
    e, n = aff_t.shape
    nblk = n // SEL_BLK
    return pl.pallas_call(
        functools.partial(_select_kernel, cap=cap, nblk=nblk),
        name="moe_select",
        out_shape=[jax.ShapeDtypeStruct((e, n), I32), jax.ShapeDtypeStruct((e, n), F32),
                   jax.ShapeDtypeStruct((nblk, e, LANES), I32)],
        compiler_params=pltpu.CompilerParams(vmem_limit_bytes=VMEM_LIMIT),
    )(aff_t)


def _window(cs_ref, e, i):
    c = cs_ref[e, i]
    c1 = cs_ref[e, i + 1]
    shift = SLOT_ALIGN.bit_length() - 1
    b0 = lax.shift_left(lax.shift_right_logical(c, shift), shift)
    npass = lax.shift_right_logical(c1 - b0 + (SLOT_WIN - 1), SLOT_WIN.bit_length() - 1)
    return b0, npass


def _compact_kernel(cs_ref, h_ref, pos_ref, xg_ref, *, cap, tiles):
    e = pl.program_id(0)
    i = pl.program_id(1)

    @pl.when(i == 0)
    def _():
        xg_ref[...] = jnp.zeros_like(xg_ref)

    rows = lax.broadcasted_iota(I32, (SLOT_WIN, SEL_BLK), 0)
    for u in range(tiles):
        b0, npass = _window(cs_ref, e, i * tiles + u)
        pos = pos_ref[0, :, u * SEL_BLK:(u + 1) * SEL_BLK]
        x = h_ref[u * SEL_BLK:(u + 1) * SEL_BLK, :]

        def body(k, _, b0=b0, pos=pos, x=x):
            lo = b0 + k * SLOT_WIN
            base = pl.multiple_of(jnp.minimum(lo, cap - SLOT_WIN), SLOT_ALIGN)
            tgt = jnp.where((pos >= lo) & (pos < lo + SLOT_WIN), pos - base, -1)
            onehot = jnp.where(rows == tgt, 1.0, 0.0).astype(BF16)
            xg_ref[0, pl.ds(base, SLOT_WIN), :] += _dot(onehot, x).astype(BF16)
            return 0

        lax.fori_loop(0, npass, body, 0)


def _compact(cs, h, pos3, cap, tiles=4):
    n, d = h.shape
    tiles = min(tiles, n // SEL_BLK)
    blk = tiles * SEL_BLK
    return pl.pallas_call(
        functools.partial(_compact_kernel, cap=cap, tiles=tiles),
        name="moe_compact",
        grid_spec=pltpu.PrefetchScalarGridSpec(
            num_scalar_prefetch=1,
            grid=(N_EXPERTS, n // blk),
            in_specs=[pl.BlockSpec((blk, d), lambda e, i, cs: (i, 0)),
                      pl.BlockSpec((1, 1, blk), lambda e, i, cs: (e, 0, i))],
            out_specs=pl.BlockSpec((1, cap, d), lambda e, i, cs: (e, 0, 0)),
        ),
        out_shape=jax.ShapeDtypeStruct((N_EXPERTS, cap, d), BF16),
        compiler_params=_params(("arbitrary",) * 2),
    )(cs, h, pos3)


def _ffn_kernel(x_ref, wg_ref, wu_ref, wd_ref, o_ref, acc_ref):
    f = pl.program_id(2)

    @pl.when(f == 0)
    def _():
        acc_ref[...] = jnp.zeros_like(acc_ref)

    x = x_ref[0]
    a = _dot(x, wg_ref[0])
    hid = (a * jax.nn.sigmoid(a)) * _dot(x, wu_ref[0])
    acc_ref[...] += _dot(hid.astype(BF16), wd_ref[0])

    @pl.when(f == pl.num_programs(2) - 1)
    def _():
        o_ref[0] = acc_ref[...].astype(BF16)


def _ffn(xg, wg, wu, wd, tm=1024, tf=512):
    e, cap, d = xg.shape
    fdim = wg.shape[2]
    tm = min(tm, cap)
    return pl.pallas_call(
        _ffn_kernel,
        name="moe_ffn",
        grid=(e, cap // tm, fdim // tf),
        in_specs=[pl.BlockSpec((1, tm, d), lambda e, m, f: (e, m, 0)),
                  pl.BlockSpec((1, d, tf), lambda e, m, f: (e, 0, f)),
                  pl.BlockSpec((1, d, tf), lambda e, m, f: (e, 0, f)),
                  pl.BlockSpec((1, tf, d), lambda e, m, f: (e, f, 0))],
        out_specs=pl.BlockSpec((1, tm, d), lambda e, m, f: (e, m, 0)),
        out_shape=jax.ShapeDtypeStruct((e, cap, d), BF16),
        scratch_shapes=[pltpu.VMEM((tm, d), F32)],
        compiler_params=_params(("arbitrary",) * 3),
    )(xg, wg, wu, wd)


def _combine_kernel(cs_ref, x_ref, gate_ref, pos_ref, om_ref, xo_ref, buf, xbuf, sem, xsem,
                    *, cap, nt):
    i = pl.program_id(0)

    def win_copy(e, t, slot):
        b0, _ = _window(cs_ref, e, t)
        base = pl.multiple_of(jnp.minimum(b0, cap - SLOT_WIN), SLOT_ALIGN)
        return pltpu.make_async_copy(om_ref.at[e, pl.ds(base, SLOT_WIN), :], buf.at[slot, e],
                                     sem.at[slot, e])

    @pl.when(i == 0)
    def _():
        for e in range(N_EXPERTS):
            win_copy(e, 0, 0).start()

    @pl.when(i + 1 < nt)
    def _():
        for e in range(N_EXPERTS):
            win_copy(e, i + 1, (i + 1) % 2).start()

    slot = i % 2
    lane = lax.broadcasted_iota(I32, (SEL_BLK, SLOT_WIN), 1)

    def gated_onehot(pcol, ghi, glo, lo):
        base = jnp.minimum(lo, cap - SLOT_WIN)
        hit = lane == jnp.where((pcol >= lo) & (pcol < lo + SLOT_WIN), pcol - base, -1)
        return jnp.where(hit, ghi, 0.0).astype(BF16), jnp.where(hit, glo, 0.0).astype(BF16)

    ecol = lax.broadcasted_iota(I32, (1, N_EXPERTS), 1)
    lo_row = jnp.zeros((1, N_EXPERTS), I32)
    for e in range(N_EXPERTS):
        win_copy(e, i, slot).wait()
        lo_row = jnp.where(ecol == e, _window(cs_ref, e, i)[0], lo_row)
    pos_all = pos_ref[...]
    gate_all = gate_ref[...]
    tgt_all = jnp.where((pos_all >= lo_row) & (pos_all < lo_row + SLOT_WIN),
                        pos_all - jnp.minimum(lo_row, cap - SLOT_WIN), -1)
    ghi_all = gate_all.astype(BF16).astype(F32)
    glo_all = gate_all - ghi_all
    his, los = [], []
    for e in range(N_EXPERTS):
        hit = lane == tgt_all[:, e:e + 1]
        his.append(jnp.where(hit, ghi_all[:, e:e + 1], 0.0).astype(BF16))
        los.append(jnp.where(hit, glo_all[:, e:e + 1], 0.0).astype(BF16))
    rows = buf[slot].reshape(N_EXPERTS * SLOT_WIN, buf.shape[-1])
    xo_ref[...] = (x_ref[...] + _dot(jnp.concatenate(his, axis=1), rows)
                   + _dot(jnp.concatenate(los, axis=1), rows))

    for e in range(N_EXPERTS):
        b0, npass = _window(cs_ref, e, i)

        def extra(k, _, e=e, b0=b0):
            lo = b0 + k * SLOT_WIN
            base = pl.multiple_of(jnp.minimum(lo, cap - SLOT_WIN), SLOT_ALIGN)
            cp = pltpu.make_async_copy(om_ref.at[e, pl.ds(base, SLOT_WIN), :], xbuf, xsem)
            cp.start()
            cp.wait()
            gcol = gate_ref[:, e:e + 1]
            ghi = gcol.astype(BF16).astype(F32)
            hi, lo_ = gated_onehot(pos_ref[:, e:e + 1], ghi, gcol - ghi, lo)
            xo_ref[...] += _dot(hi, xbuf[...]) + _dot(lo_, xbuf[...])
            return 0

        lax.fori_loop(1, npass, extra, 0)


def _combine(cs, x, gate_t, pos_t, om, cap):
    n, d = x.shape
    nt = n // SEL_BLK
    row = lambda i, cs: (i, 0)
    return pl.pallas_call(
        functools.partial(_combine_kernel, cap=cap, nt=nt),
        name="moe_combine",
        grid_spec=pltpu.PrefetchScalarGridSpec(
            num_scalar_prefetch=1,
            grid=(nt,),
            in_specs=[pl.BlockSpec((SEL_BLK, d), row),
                      pl.BlockSpec((SEL_BLK, N_EXPERTS), row),
                      pl.BlockSpec((SEL_BLK, N_EXPERTS), row),
                      pl.BlockSpec(memory_space=pl.ANY)],
            out_specs=pl.BlockSpec((SEL_BLK, d), row),
            scratch_shapes=[pltpu.VMEM((2, N_EXPERTS, SLOT_WIN, d), BF16),
                            pltpu.VMEM((SLOT_WIN, d), BF16),
                            pltpu.SemaphoreType.DMA((2, N_EXPERTS)),
                            pltpu.SemaphoreType.DMA(())],
        ),
        out_shape=jax.ShapeDtypeStruct((n, d), F32),
        compiler_params=_params(("arbitrary",)),
    )(cs, x, gate_t, pos_t, om)


def _moe(x, h, aff_t, wg, wu, wd):
    n = x.shape[0]
    cap = max(1, EC_FACTOR * n // N_EXPERTS)
    pos, gate, cs3 = _select(aff_t, cap)
    cs = jnp.concatenate([cs3[:, :, 0].T, jnp.full((N_EXPERTS, 1), cap, I32)], axis=1)
    xg = _compact(cs, h, pos.reshape(N_EXPERTS, 1, n), cap)
    om = _ffn(xg, wg, wu, wd)
    return _combine(cs, x, gate.T, pos.T, om, cap)


def _final_kernel(x_ref, g_ref, o_ref):
    o_ref[...] = _rms(x_ref[...], g_ref[...])


def _final_norm(x, g, tm=512):
    n, d = x.shape
    return pl.pallas_call(
        _final_kernel,
        name="final_norm",
        grid=(n // tm,),
        in_specs=[pl.BlockSpec((tm, d), lambda i: (i, 0)), pl.BlockSpec((1, d), lambda i: (0, 0))],
        out_specs=pl.BlockSpec((tm, d), lambda i: (i, 0)),
        out_shape=jax.ShapeDtypeStruct((n, d), F32),
        compiler_params=_params(("arbitrary",)),
    )(x, g)


def _rot_half_cols(w):
    half = w.shape[-1] // 2
    return jnp.concatenate([-w[..., half:], w[..., :half]], axis=-1)


def _pad_heads(w, heads, dim, lead=0):
    d = w.shape[0]
    w = w.reshape(d, heads, dim)
    w = jnp.pad(w, ((0, 0), (0, 0), (lead, LANES - lead - dim)))
    return w.reshape(d, heads * LANES)


def _prep_ab(w_in, w_uq, w_ukv, w_out):
    d = w_in.shape[0]
    cuts = [0, 256, 512, 1024, 1536, 1792, 1920, 1952]
    q_r, k_r, v_r, g_r, c_q, c_kv, k_pe = [w_in[:, a:b] for a, b in zip(cuts[:-1], cuts[1:])]
    q4 = q_r.reshape(d, RET_HEADS, RET_DK)
    k4 = k_r.reshape(d, RET_HEADS, RET_DK) * (RET_DK ** -0.5)
    ph = lambda w: _pad_heads(w.reshape(d, -1), RET_HEADS, RET_DK)
    kpe = jnp.pad(k_pe, ((0, 0), (MLA_NOPE, LANES - MLA_NOPE - MLA_ROPE)))
    kpe_rot = jnp.pad(_rot_half_cols(k_pe), ((0, 0), (MLA_NOPE, LANES - MLA_NOPE - MLA_ROPE)))
    w_ext = jnp.concatenate([ph(q4), ph(_rot_half_cols(q4)), ph(k4), ph(_rot_half_cols(k4)),
                             v_r, g_r, c_q, c_kv, kpe, kpe_rot], axis=1).astype(BF16)
    uq = w_uq.reshape(MLA_Q_LORA, MLA_HEADS, MLA_NOPE + MLA_ROPE)
    qa = jnp.pad(uq, ((0, 0), (0, 0), (0, LANES - MLA_NOPE - MLA_ROPE)))
    qb = jnp.pad(_rot_half_cols(uq[..., MLA_NOPE:]),
                 ((0, 0), (0, 0), (MLA_NOPE, LANES - MLA_NOPE - MLA_ROPE)))
    wuq = jnp.concatenate([qa.reshape(MLA_Q_LORA, -1), qb.reshape(MLA_Q_LORA, -1)], axis=1)
    ukv = w_ukv.reshape(MLA_KV_LORA, MLA_HEADS, MLA_NOPE + MLA_V)
    kn = jnp.pad(ukv[..., :MLA_NOPE], ((0, 0), (0, 0), (0, LANES - MLA_NOPE)))
    vv = jnp.pad(ukv[..., MLA_NOPE:], ((0, 0), (0, 0), (0, LANES - MLA_V)))
    wukv = jnp.concatenate([kn.reshape(MLA_KV_LORA, -1), vv.reshape(MLA_KV_LORA, -1)], axis=1)
    wo_a = w_out[:RET_HEADS * RET_DV]
    wo_b = w_out[RET_HEADS * RET_DV:].reshape(MLA_HEADS, MLA_V, d)
    wo_b = jnp.pad(wo_b, ((0, 0), (0, LANES - MLA_V), (0, 0))).reshape(MLA_HEADS * LANES, d)
    return w_ext, wuq.astype(BF16), wukv.astype(BF16), wo_a.astype(BF16), wo_b.astype(BF16)


def _rope_tabs(T):
    def tables(dim):
        inv = 1.0 / (ROPE_BASE ** (jnp.arange(0, dim, 2, dtype=F32) / dim))
        ang = jnp.arange(T, dtype=F32)[:, None] * inv[None, :]
        return jnp.cos(ang), jnp.sin(ang)

    cr, sr = tables(RET_DK)
    cr = jnp.tile(cr, (1, LANES // (RET_DK // 2)))
    sr = jnp.tile(sr, (1, LANES // (RET_DK // 2)))
    cm, sm = tables(MLA_ROPE)
    scale = (MLA_NOPE + MLA_ROPE) ** -0.5 * math.log2(math.e)
    ones = jnp.ones((T, MLA_NOPE), F32)
    z_lo = jnp.zeros((T, MLA_NOPE), F32)
    z_hi = jnp.zeros((T, LANES - MLA_NOPE - MLA_ROPE), F32)
    cq = jnp.concatenate([ones, cm, cm, z_hi], axis=1) * scale
    sq = jnp.concatenate([z_lo, sm, sm, z_hi], axis=1) * scale
    ck = jnp.concatenate([z_lo, cm, cm, z_hi], axis=1)
    sk = jnp.concatenate([z_lo, sm, sm, z_hi], axis=1)
    return cr, sr, cq, sq, ck, sk


def _trunk(x3, p):
    B, T, d = x3.shape
    x = x3.reshape(B * T, d)
    tabs = _rope_tabs(T)
    for l in range(DEPTH):
        j = l // 2
        g_mix = p["norm_mix"][l].reshape(1, d)
        g_ffn = p["norm_ffn"][l].reshape(1, d)
        rt = p["moe_router"][l].T
        rt_hi = rt.astype(BF16)
        router_t = jnp.concatenate([rt_hi, (rt - rt_hi.astype(F32)).astype(BF16)], axis=0)
        if l % 2 == 0:
            w_ext, wuq, wukv, wo_a, wo_b = p["ab"][j]
            qr, kr, vr, gr, q, k, v = _ab_in(x, g_mix, w_ext, wuq, wukv,
                                             p["mla_q_norm"][j].reshape(1, -1),
                                             p["mla_kv_norm"][j].reshape(1, -1), tabs, T)
            lg = jax.nn.log_sigmoid(p["ret_decay_logit"][j].astype(F32))
            y_a = _retention(lg, qr, kr, vr, gr, p["ret_gn"][j].reshape(1, -1), B, T)
            y_b = _mla(q, k, v, B, T)
            x, h, aff_t = _outproj_router(x, [y_a, y_b], [wo_a, wo_b], g_ffn, router_t)
        else:
            gate, xr = _rg_in(x, g_mix, p["rg_w_in"][j])
            y = _rg_scan(xr, gate, p["rg_conv_w"][j], p["rg_conv_b"][j].reshape(1, -1),
                         p["rg_wa"][j], p["rg_ba"][j].reshape(2, 1, -1), p["rg_wx"][j],
                         p["rg_bx"][j].reshape(2, 1, -1), p["rg_lambda"][j].reshape(2, 1, -1), B, T)
            x, h, aff_t = _outproj_router(x, [y], [p["rg_w_out"][j]], g_ffn, router_t)
        x = _moe(x, h, aff_t, p["moe_w_gate"][l], p["moe_w_up"][l], p["moe_w_down"][l])
    return _final_norm(x, p["norm_final"].reshape(1, d)).reshape(B, T, d)


def kernel(x_prompt, x_sample, norm_mix, norm_ffn, norm_final, ab_w_in, ret_decay_logit, ret_gn, mla_q_norm, mla_w_uq, mla_kv_norm, mla_w_ukv, ab_w_out, rg_w_in, rg_conv_w, rg_conv_b, rg_wa, rg_ba, rg_wx, rg_bx, rg_lambda, rg_w_out, moe_router, moe_w_gate, moe_w_up, moe_w_down):
    p = dict(
        norm_mix=norm_mix, norm_ffn=norm_ffn, norm_final=norm_final,
        ret_decay_logit=ret_decay_logit, ret_gn=ret_gn, mla_q_norm=mla_q_norm,
        mla_kv_norm=mla_kv_norm, rg_conv_w=rg_conv_w, rg_conv_b=rg_conv_b, rg_ba=rg_ba,
        rg_bx=rg_bx, rg_lambda=rg_lambda, moe_router=moe_router,
        ab=[_prep_ab(ab_w_in[j], mla_w_uq[j], mla_w_ukv[j], ab_w_out[j])
            for j in range(ab_w_in.shape[0])],
        rg_w_in=rg_w_in.astype(BF16), rg_wa=rg_wa.astype(BF16), rg_wx=rg_wx.astype(BF16),
        rg_w_out=rg_w_out.astype(BF16), moe_w_gate=moe_w_gate.astype(BF16),
        moe_w_up=moe_w_up.astype(BF16), moe_w_down=moe_w_down.astype(BF16),
    )
    return _trunk(x_prompt, p), _trunk(x_sample, p)
```

```python
import functools
import math

import jax
import jax.numpy as jnp
from jax import lax
from jax.experimental import pallas as pl
from jax.experimental.pallas import tpu as pltpu

F32 = jnp.float32
BF16 = jnp.bfloat16
I32 = jnp.int32

D_MODEL = 1024
DEPTH = 4
EPS = 1e-6
ROPE_BASE = 10000.0
RET_HEADS = 4
RET_DK = 64
RET_DV = 128
MLA_HEADS = 8
MLA_NOPE = 64
MLA_ROPE = 32
MLA_V = 64
MLA_Q_LORA = 256
MLA_KV_LORA = 128
D_RNN = D_MODEL
RG_BLOCKS = 8
RG_BLK = D_RNN // RG_BLOCKS
CONV_W = 4
CONV_LEFT = CONV_W // 2
RG_C = 8.0
N_EXPERTS = 16
EC_FACTOR = 2
D_EXPERT = 2 * D_MODEL

LANES = 128
SLOT_ALIGN = 64
SLOT_WIN = 128
VMEM_LIMIT = 56 * 1024 * 1024


def _params(sem):
    return pltpu.CompilerParams(dimension_semantics=sem, vmem_limit_bytes=VMEM_LIMIT)


def _rms(x, g):
    return x * lax.rsqrt(jnp.mean(x * x, axis=-1, keepdims=True) + EPS) * g


def _dot(a, b):
    return jnp.dot(a, b, preferred_element_type=F32)


def _dot_nt(a, b):
    return lax.dot_general(a, b, (((1,), (1,)), ((), ())), preferred_element_type=F32)


def _dot_tn(a, b):
    return lax.dot_general(a, b, (((0,), (0,)), ((), ())), preferred_element_type=F32)


AB_SEG = {
    "q": (0, 512), "q_rot": (512, 1024), "k": (1024, 1536), "k_rot": (1536, 2048),
    "v": (2048, 2560), "g": (2560, 3072), "c_q": (3072, 3328), "c_kv": (3328, 3456),
    "k_pe": (3456, 3584), "k_pe_rot": (3584, 3712),
}
AB_EXT = 3712


def _ab_in_kernel(x_ref, g_ref, w_ref, wuq_ref, wuk_ref, wvt_ref, qn_ref, kvn_ref, cr_ref, sr_ref,
                  cq_ref, sq_ref, ck_ref, sk_ref,
                  qr_out, kr_out, vr_out, gr_out, q_out, k_out, vt_out):
    hb = _rms(x_ref[...], g_ref[...]).astype(BF16)

    def seg(name):
        a, b = AB_SEG[name]
        return _dot(hb, w_ref[:, a:b])

    cr = jnp.tile(cr_ref[...], (1, RET_HEADS))
    sr = jnp.tile(sr_ref[...], (1, RET_HEADS))
    qr_out[...] = (seg("q") * cr + seg("q_rot") * sr).astype(BF16)
    kr_out[...] = (seg("k") * cr + seg("k_rot") * sr).astype(BF16)
    vr_out[...] = seg("v").astype(BF16)
    gr_out[...] = seg("g").astype(BF16)

    cqn = _rms(seg("c_q"), qn_ref[...]).astype(BF16)
    hq = MLA_HEADS * LANES
    qa = _dot(cqn, wuq_ref[:, 0:hq])
    qb = _dot(cqn, wuq_ref[:, hq:2 * hq])
    q_out[...] = (qa * jnp.tile(cq_ref[...], (1, MLA_HEADS))
                  + qb * jnp.tile(sq_ref[...], (1, MLA_HEADS))).astype(BF16)

    ckvn = _rms(seg("c_kv"), kvn_ref[...]).astype(BF16)
    kn = _dot(ckvn, wuk_ref[...])
    kpe = seg("k_pe") * ck_ref[...] + seg("k_pe_rot") * sk_ref[...]
    k_out[...] = (kn + jnp.tile(kpe, (1, MLA_HEADS))).astype(BF16)
    vt = _dot_nt(wvt_ref[...], ckvn)
    row = lax.broadcasted_iota(I32, vt.shape, 0)
    vt_out[...] = jnp.where(row % LANES == MLA_V, 1.0, vt).astype(BF16)


def _ab_in(x, g, w_ext, wuq, wuk, wvt, qn, kvn, tabs, T, tm=256):
    n, d = x.shape
    nt = T // tm
    row = lambda i: (i, 0)
    const = lambda i: (0, 0)
    tab = lambda i: (i % nt, 0)
    hq = MLA_HEADS * LANES
    hr = RET_HEADS * LANES
    outs = [jax.ShapeDtypeStruct((n, w), BF16) for w in (hr, hr, hr, hr, hq, hq)]
    outs.append(jax.ShapeDtypeStruct((hq, n), BF16))
    return pl.pallas_call(
        _ab_in_kernel,
        name="ab_in",
        grid=(n // tm,),
        in_specs=[
            pl.BlockSpec((tm, d), row),
            pl.BlockSpec((1, d), const),
            pl.BlockSpec((d, AB_EXT), const),
            pl.BlockSpec((MLA_Q_LORA, 2 * hq), const),
            pl.BlockSpec((MLA_KV_LORA, hq), const),
            pl.BlockSpec((hq, MLA_KV_LORA), const),
            pl.BlockSpec((1, MLA_Q_LORA), const),
            pl.BlockSpec((1, MLA_KV_LORA), const),
        ] + [pl.BlockSpec((tm, LANES), tab)] * 6,
        out_specs=[pl.BlockSpec((tm, w.shape[1]), row) for w in outs[:-1]]
        + [pl.BlockSpec((hq, tm), lambda i: (0, i))],
        out_shape=outs,
        compiler_params=_params(("arbitrary",)),
    )(x, g, w_ext, wuq, wuk, wvt, qn, kvn, *tabs)


def _ret_kernel(lg_ref, q_ref, k_ref, v_ref, g_ref, gn_ref, o_ref, s_ref, sb_ref, *, cs, nc):
    p = pl.program_id(1)
    c = pl.program_id(2)
    pos = lax.broadcasted_iota(I32, (cs, LANES), 0).astype(F32)

    @pl.when(c == 0)
    def _():
        s_ref[...] = jnp.zeros_like(s_ref)

    def decay(lg):
        return jnp.exp(jnp.full((1, LANES), cs, F32) * lg)

    def head(ref, h):
        return ref[:, h * LANES:(h + 1) * LANES]

    @pl.when(p == 0)
    def _():
        for h in range(RET_HEADS):
            lgb = lg_ref[1, h]
            sb_ref[nc - 1 - c, h] = s_ref[h]
            kw = (head(k_ref, h).astype(F32) * jnp.exp(pos * lgb)).astype(BF16)
            s_ref[h] = s_ref[h] * decay(lgb) + _dot_tn(kw, head(v_ref, h))

    @pl.when(p == 1)
    def _():
        ii = lax.broadcasted_iota(I32, (cs, cs), 0)
        jj = lax.broadcasted_iota(I32, (cs, cs), 1)
        dd = ii - jj
        dist = jnp.abs(dd).astype(F32)
        for h in range(RET_HEADS):
            lgf = lg_ref[0, h]
            lgb = lg_ref[1, h]
            q = head(q_ref, h)
            k = head(k_ref, h)
            v = head(v_ref, h)
            qf = q.astype(F32)
            intra = jnp.exp(dist * jnp.where(dd >= 0, lgf, lgb))
            o = _dot((_dot_nt(q, k) * intra).astype(BF16), v)
            qwf = (qf * jnp.exp((pos + 1.0) * lgf)).astype(BF16)
            qwb = (qf * jnp.exp((cs - pos) * lgb)).astype(BF16)
            o = o + _dot(qwf, s_ref[h].astype(BF16)) + _dot(qwb, sb_ref[c, h].astype(BF16))
            kw = (k.astype(F32) * jnp.exp((cs - 1.0 - pos) * lgf)).astype(BF16)
            s_ref[h] = s_ref[h] * decay(lgf) + _dot_tn(kw, v)
            o = o * lax.rsqrt(jnp.mean(o * o, axis=-1, keepdims=True) + EPS)
            g = head(g_ref, h).astype(F32)
            o_ref[:, h * LANES:(h + 1) * LANES] = (
                o * head(gn_ref, h) * (g * jax.nn.sigmoid(g))).astype(BF16)


def _retention(lg, q, k, v, g, gn, B, T, cs=256):
    n, w = q.shape
    nc = T // cs

    def chunk(p, c):
        return p * c + (1 - p) * (nc - 1 - c)

    blk = lambda b, p, c, lg: (b * nc + chunk(p, c), 0)
    return pl.pallas_call(
        functools.partial(_ret_kernel, cs=cs, nc=nc),
        name="retention",
        grid_spec=pltpu.PrefetchScalarGridSpec(
            num_scalar_prefetch=1,
            grid=(B, 2, nc),
            in_specs=[pl.BlockSpec((cs, w), blk)] * 4
            + [pl.BlockSpec((1, w), lambda b, p, c, lg: (0, 0))],
            out_specs=pl.BlockSpec((cs, w), lambda b, p, c, lg: (b * nc + p * c, 0)),
            scratch_shapes=[pltpu.VMEM((RET_HEADS, LANES, LANES), F32),
                            pltpu.VMEM((nc, RET_HEADS, LANES, LANES), F32)],
        ),
        out_shape=jax.ShapeDtypeStruct((n, w), BF16),
        compiler_params=_params(("arbitrary",) * 3),
    )(lg, q, k, v, g, gn)


def _mla_kernel(q_ref, k_ref, vt_ref, ot_ref, s_ref, p_ref, *, tk, nk, nchain):
    cols = q_ref.shape[0] // nchain
    qs = [q_ref[c * cols:(c + 1) * cols, :] for c in range(nchain)]

    def keys(j):
        return k_ref[pl.ds(pl.multiple_of(j * tk, tk), tk), :]

    def vals(j):
        return vt_ref[:, pl.ds(pl.multiple_of(j * tk, tk), tk)]

    def step(j, slot, carry):
        kn = keys(jnp.minimum(j + 1, nk - 1))
        vp = vals(jnp.maximum(j - 1, 0))
        out = []
        for c, (m, a_prev, acc) in enumerate(carry):
            s_ref[c, 1 - slot] = _dot_nt(kn, qs[c])
            acc = a_prev * acc + _dot(vp, p_ref[c, 1 - slot])
            s = s_ref[c, slot]
            m_new = jnp.maximum(m, jnp.max(s, axis=0, keepdims=True))
            p_ref[c, slot] = jnp.exp2(s - m_new).astype(BF16)
            out.append((m_new, jnp.exp2(m - m_new), acc))
        return tuple(out)

    k0 = keys(0)
    for c in range(nchain):
        s_ref[c, 0] = _dot_nt(k0, qs[c])
        p_ref[c, 1] = jnp.zeros(p_ref.shape[2:], BF16)
    init = tuple((jnp.full((1, cols), -jnp.inf, F32), jnp.ones((1, cols), F32),
                  jnp.zeros((LANES, cols), F32)) for _ in range(nchain))
    res = lax.fori_loop(0, nk // 2, lambda t, cr: step(2 * t + 1, 1, step(2 * t, 0, cr)), init)
    vl = vals(nk - 1)
    for c, (_, a_prev, acc) in enumerate(res):
        acc = a_prev * acc + _dot(vl, p_ref[c, 1])
        ot_ref[:, c * cols:(c + 1) * cols] = (acc / acc[MLA_V:MLA_V + 1, :]).astype(BF16)


def _mla(q, k, vt, B, T, tq=512, tk=1024, nchain=2):
    n = q.shape[0]
    tq = min(tq, T)
    tk = min(tk, T // 2)
    nq = T // tq
    nk = T // tk
    assert nk % 2 == 0
    cols = tq // nchain
    return pl.pallas_call(
        functools.partial(_mla_kernel, tk=tk, nk=nk, nchain=nchain),
        name="mla",
        grid=(B, MLA_HEADS, nq),
        in_specs=[
            pl.BlockSpec((tq, LANES), lambda b, h, i: (b * nq + i, h)),
            pl.BlockSpec((T, LANES), lambda b, h, i: (b, h)),
            pl.BlockSpec((LANES, T), lambda b, h, i: (h, b)),
        ],
        out_specs=pl.BlockSpec((LANES, tq), lambda b, h, i: (h, b * nq + i)),
        out_shape=jax.ShapeDtypeStruct((MLA_HEADS * LANES, n), BF16),
        scratch_shapes=[pltpu.VMEM((nchain, 2, tk, cols), F32), pltpu.VMEM((nchain, 2, tk, cols), BF16)],
        compiler_params=_params(("arbitrary",) * 3),
    )(q, k, vt)


def _outproj_kernel(*refs, npairs, transposed):
    x_ref = refs[0]
    ys = refs[1:1 + npairs]
    ws = refs[1 + npairs:1 + 2 * npairs]
    g_ref, rt_ref, xo_ref, h_ref, aff_ref = refs[1 + 2 * npairs:]
    half = x_ref.shape[0] // 2
    for r in (slice(0, half), slice(half, 2 * half)):
        acc = x_ref[r, :]
        for y, w, t in zip(ys, ws, transposed):
            acc = acc + (_dot_tn(y[:, r], w[...]) if t else _dot(y[r, :], w[...]))
        xo_ref[r, :] = acc
        h = _rms(acc, g_ref[...])
        h_hi = h.astype(BF16)
        h_ref[r, :] = h_hi
        part = _dot_nt(rt_ref[...], h_hi)
        h_lo = (h - h_hi.astype(F32)).astype(BF16)
        logits = (part[0:N_EXPERTS] + part[N_EXPERTS:2 * N_EXPERTS]
                  + _dot_nt(rt_ref[0:N_EXPERTS, :], h_lo))
        e = jnp.exp(logits - jnp.max(logits, axis=0, keepdims=True))
        aff_ref[:, r] = e / jnp.sum(e, axis=0, keepdims=True)


def _outproj_router(x, ys, ws, g, router_t, transposed=None, tm=512):
    n, d = x.shape
    transposed = tuple(transposed or (False,) * len(ys))
    row = lambda i: (i, 0)
    const = lambda i: (0, 0)
    return pl.pallas_call(
        functools.partial(_outproj_kernel, npairs=len(ys), transposed=transposed),
        name="outproj_router",
        grid=(n // tm,),
        in_specs=[pl.BlockSpec((tm, d), row)]
        + [pl.BlockSpec((y.shape[0], tm), lambda i: (0, i)) if t
           else pl.BlockSpec((tm, y.shape[1]), row) for y, t in zip(ys, transposed)]
        + [pl.BlockSpec(w.shape, const) for w in ws]
        + [pl.BlockSpec((1, d), const), pl.BlockSpec((2 * N_EXPERTS, d), const)],
        out_specs=[pl.BlockSpec((tm, d), row), pl.BlockSpec((tm, d), row),
                   pl.BlockSpec((N_EXPERTS, tm), lambda i: (0, i))],
        out_shape=[jax.ShapeDtypeStruct((n, d), F32), jax.ShapeDtypeStruct((n, d), BF16),
                   jax.ShapeDtypeStruct((N_EXPERTS, n), F32)],
        compiler_params=_params(("arbitrary",)),
    )(x, *ys, *ws, g, router_t)


def _rg_in_kernel(x_ref, g_ref, w_ref, gate_out, xr_out):
    hb = _rms(x_ref[...], g_ref[...]).astype(BF16)
    gate_out[...] = _dot(hb, w_ref[:, 0:D_RNN]).astype(BF16)
    xr_out[...] = _dot(hb, w_ref[:, D_RNN:2 * D_RNN])


def _rg_in(x, g, w, tm=256):
    n, d = x.shape
    row = lambda i: (i, 0)
    const = lambda i: (0, 0)
    return pl.pallas_call(
        _rg_in_kernel,
        name="rg_in",
        grid=(n // tm,),
        in_specs=[pl.BlockSpec((tm, d), row), pl.BlockSpec((1, d), const),
                  pl.BlockSpec((d, 2 * D_RNN), const)],
        out_specs=[pl.BlockSpec((tm, D_RNN), row), pl.BlockSpec((tm, D_RNN), row)],
        out_shape=[jax.ShapeDtypeStruct((n, D_RNN), BF16), jax.ShapeDtypeStruct((n, D_RNN), F32)],
        compiler_params=_params(("arbitrary",)),
    )(x, g, w)


HALO = 8
RG_LANES = 512
SUBLANES = 8


def _rg_scan_kernel(xr_ref, prev_ref, next_ref, gate_ref, cw_ref, cb_ref, wa_ref, ba_ref, wx_ref,
                    bx_ref, lam_ref, y_ref, carry_ref, hb_ref, *, tc, nt):
    p = pl.program_id(2)
    c = pl.program_id(3)
    chunk = p * c + (1 - p) * (nt - 1 - c)

    @pl.when(c == 0)
    def _():
        carry_ref[...] = jnp.zeros_like(carry_ref)

    cur = xr_ref[...]
    prev = jnp.where(chunk > 0, prev_ref[...], 0.0)
    nxt = jnp.where(chunk < nt - 1, next_ref[...], 0.0)
    ext = jnp.concatenate([prev, cur, nxt], axis=0)
    xc = cb_ref[...]
    for j in range(CONV_W):
        o = HALO - CONV_LEFT + j
        xc = xc + ext[o:o + tc, :] * cw_ref[j:j + 1, :]
    xcb = xc.astype(BF16)

    def block_diag(w_ref):
        return jnp.concatenate(
            [_dot(xcb[:, k * RG_BLK:(k + 1) * RG_BLK], w_ref[0, k])
             for k in range(RG_LANES // RG_BLK)], axis=1)

    r = jax.nn.sigmoid(block_diag(wa_ref) + ba_ref[0])
    gi = jax.nn.sigmoid(block_diag(wx_ref) + bx_ref[0])
    nl = -lam_ref[0]
    softplus = jnp.maximum(nl, 0.0) + jnp.log1p(jnp.exp(-jnp.abs(nl)))
    log_a = -RG_C * r * softplus
    a = jnp.exp(log_a)
    om = 1.0 - a * a
    b = jnp.where(om > 0.0, om * lax.rsqrt(om), 0.0) * (gi * xc)
    sub = lax.broadcasted_iota(I32, (tc, RG_LANES), 0) % SUBLANES

    def scan(a, b, reverse):
        groups = tc // SUBLANES
        shape3 = (groups, SUBLANES, RG_LANES)
        a = a.reshape(shape3)
        b = b.reshape(shape3)
        sub3 = sub.reshape(shape3)
        for s in (1, 2, 4):
            keep = (sub3 < SUBLANES - s) if reverse else (sub3 >= s)
            sh = SUBLANES - s if reverse else s
            a_sh = jnp.where(keep, pltpu.roll(a, sh, 1), 1.0)
            b_sh = jnp.where(keep, pltpu.roll(b, sh, 1), 0.0)
            b = a * b_sh + b
            a = a * a_sh
        a = a.reshape(tc, RG_LANES)
        b = b.reshape(tc, RG_LANES)
        out = [None] * groups
        carry = carry_ref[0:1, :]
        for g in (range(groups - 1, -1, -1) if reverse else range(groups)):
            rows = slice(g * SUBLANES, (g + 1) * SUBLANES)
            out[g] = a[rows] * carry + b[rows]
            carry = out[g][0:1] if reverse else out[g][SUBLANES - 1:SUBLANES]
        carry_ref[...] = jnp.broadcast_to(carry, carry_ref.shape)
        return jnp.concatenate(out, axis=0)

    @pl.when(p == 0)
    def _():
        hb_ref[chunk] = scan(a, b, True).astype(BF16)

    @pl.when(p == 1)
    def _():
        hs = scan(a, b, False) + hb_ref[chunk].astype(F32)
        y_ref[...] = (jax.nn.gelu(gate_ref[...].astype(F32)) * hs).astype(BF16)


def _rg_scan(xr, gate, cw, cb, wa, ba, wx, bx, lam, B, T, tc=256):
    n = xr.shape[0]
    tc = min(tc, T)
    nt = T // tc
    hpc = tc // HALO
    gb = RG_LANES // RG_BLK

    def chunk(p, c):
        return p * c + (1 - p) * (nt - 1 - c)

    cur = lambda b, j, p, c: (b * nt + chunk(p, c), j)
    prv = lambda b, j, p, c: (jnp.maximum((b * nt + chunk(p, c)) * hpc - 1, 0), j)
    nxt = lambda b, j, p, c: (jnp.minimum((b * nt + chunk(p, c) + 1) * hpc, n // HALO - 1), j)
    par = lambda b, j, p, c: (1 - p, j, 0, 0)
    vec = lambda b, j, p, c: (1 - p, 0, j)
    return pl.pallas_call(
        functools.partial(_rg_scan_kernel, tc=tc, nt=nt),
        name="rg_scan",
        grid=(B, D_RNN // RG_LANES, 2, nt),
        in_specs=[
            pl.BlockSpec((tc, RG_LANES), cur),
            pl.BlockSpec((HALO, RG_LANES), prv),
            pl.BlockSpec((HALO, RG_LANES), nxt),
            pl.BlockSpec((tc, RG_LANES), cur),
            pl.BlockSpec((CONV_W, RG_LANES), lambda b, j, p, c: (0, j)),
            pl.BlockSpec((1, RG_LANES), lambda b, j, p, c: (0, j)),
            pl.BlockSpec((1, gb, RG_BLK, RG_BLK), par),
            pl.BlockSpec((1, 1, RG_LANES), vec),
            pl.BlockSpec((1, gb, RG_BLK, RG_BLK), par),
            pl.BlockSpec((1, 1, RG_LANES), vec),
            pl.BlockSpec((1, 1, RG_LANES), vec),
        ],
        out_specs=pl.BlockSpec((tc, RG_LANES), lambda b, j, p, c: (b * nt + p * c, j)),
        out_shape=jax.ShapeDtypeStruct((n, D_RNN), BF16),
        scratch_shapes=[pltpu.VMEM((HALO, RG_LANES), F32), pltpu.VMEM((nt, tc, RG_LANES), BF16)],
        compiler_params=_params(("arbitrary",) * 4),
    )(xr, xr, xr, gate, cw, cb, wa, ba, wx, bx, lam)


SEL_BLK = 256


def _select_kernel(aff_ref, pos_ref, gate_ref, cs_ref, *, cap, nblk):
    keys = pltpu.bitcast(aff_ref[...], I32)

    def search(i, thr):
        cand = thr | lax.shift_left(jnp.int32(1), 30 - i)
        cnt = jnp.sum(jnp.where(keys >= cand, 1.0, 0.0), axis=1, keepdims=True)
        return jnp.where(cnt >= cap, cand, thr)

    thr = lax.fori_loop(0, 31, search, jnp.zeros((N_EXPERTS, 1), I32))
    need = cap - jnp.sum(jnp.where(keys > thr, 1.0, 0.0), axis=1, keepdims=True)
    ri = lax.broadcasted_iota(I32, (SEL_BLK, SEL_BLK), 0)
    ci = lax.broadcasted_iota(I32, (SEL_BLK, SEL_BLK), 1)
    tri = jnp.where(ri <= ci, 1.0, 0.0).astype(BF16)

    def body(j, carry):
        ceq, csel = carry
        off = pl.multiple_of(j * SEL_BLK, SEL_BLK)
        a = aff_ref[:, pl.ds(off, SEL_BLK)]
        kk = pltpu.bitcast(a, I32)
        gt = kk > thr
        eq = kk == thr
        eqf = jnp.where(eq, 1.0, 0.0)
        eqc = _dot(eqf.astype(BF16), tri) + ceq
        sel = gt | (eq & (eqc <= need))
        self_ = jnp.where(sel, 1.0, 0.0)
        selc = _dot(self_.astype(BF16), tri) + csel
        pos_ref[:, pl.ds(off, SEL_BLK)] = jnp.where(sel, selc - 1.0, -1.0).astype(I32)
        gate_ref[:, pl.ds(off, SEL_BLK)] = jnp.where(sel, a, 0.0)
        cs_ref[j] = jnp.broadcast_to(csel, (N_EXPERTS, LANES)).astype(I32)
        return (ceq + jnp.sum(eqf, axis=1, keepdims=True),
                csel + jnp.sum(self_, axis=1, keepdims=True))

    z = jnp.zeros((N_EXPERTS, 1), F32)
    lax.fori_loop(0, nblk, body, (z, z))


def _select(aff_t, cap):
    e, n = aff_t.shape
    nblk = n // SEL_BLK
    return pl.pallas_call(
        functools.partial(_select_kernel, cap=cap, nblk=nblk),
        name="moe_select",
        out_shape=[jax.ShapeDtypeStruct((e, n), I32), jax.ShapeDtypeStruct((e, n), F32),
                   jax.ShapeDtypeStruct((nblk, e, LANES), I32)],
        compiler_params=pltpu.CompilerParams(vmem_limit_bytes=VMEM_LIMIT),
    )(aff_t)


def _window(cs_ref, e, i):
    c = cs_ref[e, i]
    c1 = cs_ref[e, i + 1]
    shift = SLOT_ALIGN.bit_length() - 1
    b0 = lax.shift_left(lax.shift_right_logical(c, shift), shift)
    npass = lax.shift_right_logical(c1 - b0 + (SLOT_WIN - 1), SLOT_WIN.bit_length() - 1)
    return b0, npass


def _compact_kernel(cs_ref, h_ref, pos_ref, xg_ref, *, cap, tiles):
    e = pl.program_id(0)
    i = pl.program_id(1)

    @pl.when(i == 0)
    def _():
        xg_ref[...] = jnp.zeros_like(xg_ref)

    rows = lax.broadcasted_iota(I32, (SLOT_WIN, SEL_BLK), 0)

    def place(u, b0, k):
        lo = b0 + k * SLOT_WIN
        base = pl.multiple_of(jnp.minimum(lo, cap - SLOT_WIN), SLOT_ALIGN)
        pos = pos_ref[0, :, u * SEL_BLK:(u + 1) * SEL_BLK]
        tgt = jnp.where((pos >= lo) & (pos < lo + SLOT_WIN), pos - base, -1)
        onehot = jnp.where(rows == tgt, 1.0, 0.0).astype(BF16)
        x = h_ref[u * SEL_BLK:(u + 1) * SEL_BLK, :]
        xg_ref[0, pl.ds(base, SLOT_WIN), :] += _dot(onehot, x).astype(BF16)

    wins = [_window(cs_ref, e, i * tiles + u) for u in range(tiles)]
    for u, (b0, _) in enumerate(wins):
        place(u, b0, 0)
    for u, (b0, npass) in enumerate(wins):
        lax.fori_loop(1, npass, lambda k, c, u=u, b0=b0: (place(u, b0, k), c)[1], 0)


def _compact(cs, h, pos3, cap, tiles=4):
    n, d = h.shape
    tiles = min(tiles, n // SEL_BLK)
    blk = tiles * SEL_BLK
    return pl.pallas_call(
        functools.partial(_compact_kernel, cap=cap, tiles=tiles),
        name="moe_compact",
        grid_spec=pltpu.PrefetchScalarGridSpec(
            num_scalar_prefetch=1,
            grid=(N_EXPERTS, n // blk),
            in_specs=[pl.BlockSpec((blk, d), lambda e, i, cs: (i, 0)),
                      pl.BlockSpec((1, 1, blk), lambda e, i, cs: (e, 0, i))],
            out_specs=pl.BlockSpec((1, cap, d), lambda e, i, cs: (e, 0, 0)),
        ),
        out_shape=jax.ShapeDtypeStruct((N_EXPERTS, cap, d), BF16),
        compiler_params=_params(("arbitrary",) * 2),
    )(cs, h, pos3)


def _ffn_kernel(x_ref, wg_ref, wu_ref, wd_ref, o_ref, acc_ref):
    f = pl.program_id(2)

    @pl.when(f == 0)
    def _():
        acc_ref[...] = jnp.zeros_like(acc_ref)

    x = x_ref[0]
    a = _dot(x, wg_ref[0, 0].astype(BF16))
    hid = (a * jax.nn.sigmoid(a)) * _dot(x, wu_ref[0, 0].astype(BF16))
    acc_ref[...] += _dot(hid.astype(BF16), wd_ref[0, 0].astype(BF16))

    @pl.when(f == pl.num_programs(2) - 1)
    def _():
        o_ref[0] = acc_ref[...].astype(BF16)


def _ffn(xg, wg, wu, wd, layer, tm=1024, tf=512):
    e, cap, d = xg.shape
    fdim = wg.shape[3]
    tm = min(tm, cap)
    return pl.pallas_call(
        _ffn_kernel,
        name="moe_ffn",
        grid=(e, cap // tm, fdim // tf),
        in_specs=[pl.BlockSpec((1, tm, d), lambda e, m, f: (e, m, 0)),
                  pl.BlockSpec((1, 1, d, tf), lambda e, m, f: (layer, e, 0, f)),
                  pl.BlockSpec((1, 1, d, tf), lambda e, m, f: (layer, e, 0, f)),
                  pl.BlockSpec((1, 1, tf, d), lambda e, m, f: (layer, e, f, 0))],
        out_specs=pl.BlockSpec((1, tm, d), lambda e, m, f: (e, m, 0)),
        out_shape=jax.ShapeDtypeStruct((e, cap, d), BF16),
        scratch_shapes=[pltpu.VMEM((tm, d), F32)],
        compiler_params=_params(("arbitrary",) * 3),
    )(xg, wg, wu, wd)


def _combine_kernel(cs_ref, x_ref, gate_ref, pos_ref, om_ref, xo_ref, buf, xbuf, sem, xsem,
                    *, cap, nt):
    i = pl.program_id(0)

    def win_copy(e, t, slot):
        b0, _ = _window(cs_ref, e, t)
        base = pl.multiple_of(jnp.minimum(b0, cap - SLOT_WIN), SLOT_ALIGN)
        return pltpu.make_async_copy(om_ref.at[e, pl.ds(base, SLOT_WIN), :], buf.at[slot, e],
                                     sem.at[slot, e])

    @pl.when(i == 0)
    def _():
        for e in range(N_EXPERTS):
            win_copy(e, 0, 0).start()

    @pl.when(i + 1 < nt)
    def _():
        for e in range(N_EXPERTS):
            win_copy(e, i + 1, (i + 1) % 2).start()

    slot = i % 2
    lane = lax.broadcasted_iota(I32, (SEL_BLK, SLOT_WIN), 1)

    ecol = lax.broadcasted_iota(I32, (1, N_EXPERTS), 1)
    lo_row = jnp.zeros((1, N_EXPERTS), I32)
    for e in range(N_EXPERTS):
        win_copy(e, i, slot).wait()
        lo_row = jnp.where(ecol == e, _window(cs_ref, e, i)[0], lo_row)
    pos_all = pos_ref[...]
    gate_all = gate_ref[...]
    tgt_all = jnp.where((pos_all >= lo_row) & (pos_all < lo_row + SLOT_WIN),
                        pos_all - jnp.minimum(lo_row, cap - SLOT_WIN), -1)
    onehot = jnp.concatenate(
        [jnp.where(lane == tgt_all[:, e:e + 1], gate_all[:, e:e + 1], 0.0).astype(BF16)
         for e in range(N_EXPERTS)], axis=1)
    rows = buf[slot].reshape(N_EXPERTS * SLOT_WIN, buf.shape[-1])
    xo_ref[...] = x_ref[...] + _dot(onehot, rows)

    for e in range(N_EXPERTS):
        b0, npass = _window(cs_ref, e, i)

        def extra(k, _, e=e, b0=b0):
            lo = b0 + k * SLOT_WIN
            base = pl.multiple_of(jnp.minimum(lo, cap - SLOT_WIN), SLOT_ALIGN)
            cp = pltpu.make_async_copy(om_ref.at[e, pl.ds(base, SLOT_WIN), :], xbuf, xsem)
            cp.start()
            cp.wait()
            pcol = pos_ref[:, e:e + 1]
            tgt = jnp.where((pcol >= lo) & (pcol < lo + SLOT_WIN), pcol - base, -1)
            onehot = jnp.where(lane == tgt, gate_ref[:, e:e + 1], 0.0).astype(BF16)
            xo_ref[...] += _dot(onehot, xbuf[...])
            return 0

        lax.fori_loop(1, npass, extra, 0)


def _combine(cs, x, gate_t, pos_t, om, cap):
    n, d = x.shape
    nt = n // SEL_BLK
    row = lambda i, cs: (i, 0)
    return pl.pallas_call(
        functools.partial(_combine_kernel, cap=cap, nt=nt),
        name="moe_combine",
        grid_spec=pltpu.PrefetchScalarGridSpec(
            num_scalar_prefetch=1,
            grid=(nt,),
            in_specs=[pl.BlockSpec((SEL_BLK, d), row),
                      pl.BlockSpec((SEL_BLK, N_EXPERTS), row),
                      pl.BlockSpec((SEL_BLK, N_EXPERTS), row),
                      pl.BlockSpec(memory_space=pl.ANY)],
            out_specs=pl.BlockSpec((SEL_BLK, d), row),
            scratch_shapes=[pltpu.VMEM((2, N_EXPERTS, SLOT_WIN, d), BF16),
                            pltpu.VMEM((SLOT_WIN, d), BF16),
                            pltpu.SemaphoreType.DMA((2, N_EXPERTS)),
                            pltpu.SemaphoreType.DMA(())],
        ),
        out_shape=jax.ShapeDtypeStruct((n, d), F32),
        compiler_params=_params(("arbitrary",)),
    )(cs, x, gate_t, pos_t, om)


def _moe(x, h, aff_t, wg, wu, wd, layer):
    n = x.shape[0]
    cap = max(1, EC_FACTOR * n // N_EXPERTS)
    pos, gate, cs3 = _select(aff_t, cap)
    cs = jnp.concatenate([cs3[:, :, 0].T, jnp.full((N_EXPERTS, 1), cap, I32)], axis=1)
    xg = _compact(cs, h, pos.reshape(N_EXPERTS, 1, n), cap)
    om = _ffn(xg, wg, wu, wd, layer)
    return _combine(cs, x, gate.T, pos.T, om, cap)


def _final_kernel(x_ref, g_ref, o_ref):
    o_ref[...] = _rms(x_ref[...], g_ref[...])


def _final_norm(x, g, tm=512):
    n, d = x.shape
    return pl.pallas_call(
        _final_kernel,
        name="final_norm",
        grid=(n // tm,),
        in_specs=[pl.BlockSpec((tm, d), lambda i: (i, 0)), pl.BlockSpec((1, d), lambda i: (0, 0))],
        out_specs=pl.BlockSpec((tm, d), lambda i: (i, 0)),
        out_shape=jax.ShapeDtypeStruct((n, d), F32),
        compiler_params=_params(("arbitrary",)),
    )(x, g)


def _rot_half_cols(w):
    half = w.shape[-1] // 2
    return jnp.concatenate([-w[..., half:], w[..., :half]], axis=-1)


def _pad_heads(w, heads, dim, lead=0):
    d = w.shape[0]
    w = w.reshape(d, heads, dim)
    w = jnp.pad(w, ((0, 0), (0, 0), (lead, LANES - lead - dim)))
    return w.reshape(d, heads * LANES)


def _prep_ab(w_in, w_uq, w_ukv, w_out):
    d = w_in.shape[0]
    cuts = [0, 256, 512, 1024, 1536, 1792, 1920, 1952]
    q_r, k_r, v_r, g_r, c_q, c_kv, k_pe = [w_in[:, a:b] for a, b in zip(cuts[:-1], cuts[1:])]
    q4 = q_r.reshape(d, RET_HEADS, RET_DK)
    k4 = k_r.reshape(d, RET_HEADS, RET_DK) * (RET_DK ** -0.5)
    ph = lambda w: _pad_heads(w.reshape(d, -1), RET_HEADS, RET_DK)
    kpe = jnp.pad(k_pe, ((0, 0), (MLA_NOPE, LANES - MLA_NOPE - MLA_ROPE)))
    kpe_rot = jnp.pad(_rot_half_cols(k_pe), ((0, 0), (MLA_NOPE, LANES - MLA_NOPE - MLA_ROPE)))
    w_ext = jnp.concatenate([ph(q4), ph(_rot_half_cols(q4)), ph(k4), ph(_rot_half_cols(k4)),
                             v_r, g_r, c_q, c_kv, kpe, kpe_rot], axis=1).astype(BF16)
    uq = w_uq.reshape(MLA_Q_LORA, MLA_HEADS, MLA_NOPE + MLA_ROPE)
    qa = jnp.pad(uq, ((0, 0), (0, 0), (0, LANES - MLA_NOPE - MLA_ROPE)))
    qb = jnp.pad(_rot_half_cols(uq[..., MLA_NOPE:]),
                 ((0, 0), (0, 0), (MLA_NOPE, LANES - MLA_NOPE - MLA_ROPE)))
    wuq = jnp.concatenate([qa.reshape(MLA_Q_LORA, -1), qb.reshape(MLA_Q_LORA, -1)], axis=1)
    ukv = w_ukv.reshape(MLA_KV_LORA, MLA_HEADS, MLA_NOPE + MLA_V)
    kn = jnp.pad(ukv[..., :MLA_NOPE], ((0, 0), (0, 0), (0, LANES - MLA_NOPE)))
    vv = jnp.pad(ukv[..., MLA_NOPE:], ((0, 0), (0, 0), (0, LANES - MLA_V)))
    wuk = kn.reshape(MLA_KV_LORA, -1)
    wvt = vv.reshape(MLA_KV_LORA, -1).T
    wo_a = w_out[:RET_HEADS * RET_DV]
    wo_b = w_out[RET_HEADS * RET_DV:].reshape(MLA_HEADS, MLA_V, d)
    wo_b = jnp.pad(wo_b, ((0, 0), (0, LANES - MLA_V), (0, 0))).reshape(MLA_HEADS * LANES, d)
    return (w_ext, wuq.astype(BF16), wuk.astype(BF16), wvt.astype(BF16), wo_a.astype(BF16),
            wo_b.astype(BF16))


def _rope_tabs(T):
    def tables(dim):
        inv = 1.0 / (ROPE_BASE ** (jnp.arange(0, dim, 2, dtype=F32) / dim))
        ang = jnp.arange(T, dtype=F32)[:, None] * inv[None, :]
        return jnp.cos(ang), jnp.sin(ang)

    cr, sr = tables(RET_DK)
    cr = jnp.tile(cr, (1, LANES // (RET_DK // 2)))
    sr = jnp.tile(sr, (1, LANES // (RET_DK // 2)))
    cm, sm = tables(MLA_ROPE)
    scale = (MLA_NOPE + MLA_ROPE) ** -0.5 * math.log2(math.e)
    ones = jnp.ones((T, MLA_NOPE), F32)
    z_lo = jnp.zeros((T, MLA_NOPE), F32)
    z_hi = jnp.zeros((T, LANES - MLA_NOPE - MLA_ROPE), F32)
    cq = jnp.concatenate([ones, cm, cm, z_hi], axis=1) * scale
    sq = jnp.concatenate([z_lo, sm, sm, z_hi], axis=1) * scale
    ck = jnp.concatenate([z_lo, cm, cm, z_hi], axis=1)
    sk = jnp.concatenate([z_lo, sm, sm, z_hi], axis=1)
    return cr, sr, cq, sq, ck, sk


def _trunk(x3, p):
    B, T, d = x3.shape
    x = x3.reshape(B * T, d)
    tabs = _rope_tabs(T)
    for l in range(DEPTH):
        j = l // 2
        g_mix = p["norm_mix"][l].reshape(1, d)
        g_ffn = p["norm_ffn"][l].reshape(1, d)
        rt = p["moe_router"][l].T
        rt_hi = rt.astype(BF16)
        router_t = jnp.concatenate([rt_hi, (rt - rt_hi.astype(F32)).astype(BF16)], axis=0)
        if l % 2 == 0:
            w_ext, wuq, wuk, wvt, wo_a, wo_b = p["ab"][j]
            qr, kr, vr, gr, q, k, vt = _ab_in(x, g_mix, w_ext, wuq, wuk, wvt,
                                             p["mla_q_norm"][j].reshape(1, -1),
                                             p["mla_kv_norm"][j].reshape(1, -1), tabs, T)
            lg = jax.nn.log_sigmoid(p["ret_decay_logit"][j].astype(F32))
            y_a = _retention(lg, qr, kr, vr, gr, p["ret_gn"][j].reshape(1, -1), B, T)
            y_bt = _mla(q, k, vt, B, T)
            x, h, aff_t = _outproj_router(x, [y_a, y_bt], [wo_a, wo_b], g_ffn, router_t,
                                          transposed=(False, True))
        else:
            gate, xr = _rg_in(x, g_mix, p["rg_w_in"][j])
            y = _rg_scan(xr, gate, p["rg_conv_w"][j], p["rg_conv_b"][j].reshape(1, -1),
                         p["rg_wa"][j], p["rg_ba"][j].reshape(2, 1, -1), p["rg_wx"][j],
                         p["rg_bx"][j].reshape(2, 1, -1), p["rg_lambda"][j].reshape(2, 1, -1), B, T)
            x, h, aff_t = _outproj_router(x, [y], [p["rg_w_out"][j]], g_ffn, router_t)
        x = _moe(x, h, aff_t, p["moe_w_gate"], p["moe_w_up"], p["moe_w_down"], l)
    return _final_norm(x, p["norm_final"].reshape(1, d)).reshape(B, T, d)


def kernel(x_prompt, x_sample, norm_mix, norm_ffn, norm_final, ab_w_in, ret_decay_logit, ret_gn, mla_q_norm, mla_w_uq, mla_kv_norm, mla_w_ukv, ab_w_out, rg_w_in, rg_conv_w, rg_conv_b, rg_wa, rg_ba, rg_wx, rg_bx, rg_lambda, rg_w_out, moe_router, moe_w_gate, moe_w_up, moe_w_down):
    p = dict(
        norm_mix=norm_mix, norm_ffn=norm_ffn, norm_final=norm_final,
        ret_decay_logit=ret_decay_logit, ret_gn=ret_gn, mla_q_norm=mla_q_norm,
        mla_kv_norm=mla_kv_norm, rg_conv_w=rg_conv_w, rg_conv_b=rg_conv_b, rg_ba=rg_ba,
        rg_bx=rg_bx, rg_lambda=rg_lambda, moe_router=moe_router,
        ab=[_prep_ab(ab_w_in[j], mla_w_uq[j], mla_w_ukv[j], ab_w_out[j])
            for j in range(ab_w_in.shape[0])],
        rg_w_in=rg_w_in.astype(BF16), rg_wa=rg_wa.astype(BF16), rg_wx=rg_wx.astype(BF16),
        rg_w_out=rg_w_out.astype(BF16), moe_w_gate=moe_w_gate, moe_w_up=moe_w_up,
        moe_w_down=moe_w_down,
    )
    return _trunk(x_prompt, p), _trunk(x_sample, p)
```

```python
import functools
import math

import jax
import jax.numpy as jnp
from jax import lax
from jax.experimental import pallas as pl
from jax.experimental.pallas import tpu as pltpu

F32 = jnp.float32
BF16 = jnp.bfloat16
I32 = jnp.int32

D_MODEL = 1024
DEPTH = 4
EPS = 1e-6
ROPE_BASE = 10000.0
RET_HEADS = 4
RET_DK = 64
RET_DV = 128
MLA_HEADS = 8
MLA_NOPE = 64
MLA_ROPE = 32
MLA_V = 64
MLA_Q_LORA = 256
MLA_KV_LORA = 128
D_RNN = D_MODEL
RG_BLOCKS = 8
RG_BLK = D_RNN // RG_BLOCKS
CONV_W = 4
CONV_LEFT = CONV_W // 2
RG_C = 8.0
N_EXPERTS = 16
EC_FACTOR = 2
D_EXPERT = 2 * D_MODEL

LANES = 128
SLOT_ALIGN = 64
SLOT_WIN = 128
VMEM_LIMIT = 56 * 1024 * 1024


def _params(sem):
    return pltpu.CompilerParams(dimension_semantics=sem, vmem_limit_bytes=VMEM_LIMIT)


def _rms(x, g):
    return x * lax.rsqrt(jnp.mean(x * x, axis=-1, keepdims=True) + EPS) * g


def _dot(a, b):
    return jnp.dot(a, b, preferred_element_type=F32)


def _dot_nt(a, b):
    return lax.dot_general(a, b, (((1,), (1,)), ((), ())), preferred_element_type=F32)


def _dot_tn(a, b):
    return lax.dot_general(a, b, (((0,), (0,)), ((), ())), preferred_element_type=F32)


AB_SEG = {
    "q": (0, 512), "q_rot": (512, 1024), "k": (1024, 1536), "k_rot": (1536, 2048),
    "v": (2048, 2560), "g": (2560, 3072), "c_q": (3072, 3328), "c_kv": (3328, 3456),
    "k_pe": (3456, 3584), "k_pe_rot": (3584, 3712),
}
AB_EXT = 3712


def _ab_in_kernel(x_ref, g_ref, w_ref, wuq_ref, wuk_ref, wvt_ref, qn_ref, kvn_ref, cr_ref, sr_ref,
                  cq_ref, sq_ref, ck_ref, sk_ref,
                  qr_out, kr_out, vr_out, gr_out, q_out, k_out, vt_out):
    hb = _rms(x_ref[...], g_ref[...]).astype(BF16)

    def seg(name):
        a, b = AB_SEG[name]
        return _dot(hb, w_ref[:, a:b])

    cr = jnp.tile(cr_ref[...], (1, RET_HEADS))
    sr = jnp.tile(sr_ref[...], (1, RET_HEADS))
    qr_out[...] = (seg("q") * cr + seg("q_rot") * sr).astype(BF16)
    kr_out[...] = (seg("k") * cr + seg("k_rot") * sr).astype(BF16)
    vr_out[...] = seg("v").astype(BF16)
    gr_out[...] = seg("g").astype(BF16)

    cqn = _rms(seg("c_q"), qn_ref[...]).astype(BF16)
    hq = MLA_HEADS * LANES
    qa = _dot(cqn, wuq_ref[:, 0:hq])
    qb = _dot(cqn, wuq_ref[:, hq:2 * hq])
    q_out[...] = (qa * jnp.tile(cq_ref[...], (1, MLA_HEADS))
                  + qb * jnp.tile(sq_ref[...], (1, MLA_HEADS))).astype(BF16)

    ckvn = _rms(seg("c_kv"), kvn_ref[...]).astype(BF16)
    kn = _dot(ckvn, wuk_ref[...])
    kpe = seg("k_pe") * ck_ref[...] + seg("k_pe_rot") * sk_ref[...]
    k_out[...] = (kn + jnp.tile(kpe, (1, MLA_HEADS))).astype(BF16)
    vt = _dot_nt(wvt_ref[...], ckvn)
    row = lax.broadcasted_iota(I32, vt.shape, 0)
    vt_out[...] = jnp.where(row % LANES == MLA_V, 1.0, vt).astype(BF16)


def _ab_in(x, g, w_ext, wuq, wuk, wvt, qn, kvn, tabs, T, tm=256):
    n, d = x.shape
    nt = T // tm
    row = lambda i: (i, 0)
    const = lambda i: (0, 0)
    tab = lambda i: (i % nt, 0)
    hq = MLA_HEADS * LANES
    hr = RET_HEADS * LANES
    outs = [jax.ShapeDtypeStruct((n, w), BF16) for w in (hr, hr, hr, hr, hq, hq)]
    outs.append(jax.ShapeDtypeStruct((hq, n), BF16))
    return pl.pallas_call(
        _ab_in_kernel,
        name="ab_in",
        grid=(n // tm,),
        in_specs=[
            pl.BlockSpec((tm, d), row),
            pl.BlockSpec((1, d), const),
            pl.BlockSpec((d, AB_EXT), const),
            pl.BlockSpec((MLA_Q_LORA, 2 * hq), const),
            pl.BlockSpec((MLA_KV_LORA, hq), const),
            pl.BlockSpec((hq, MLA_KV_LORA), const),
            pl.BlockSpec((1, MLA_Q_LORA), const),
            pl.BlockSpec((1, MLA_KV_LORA), const),
        ] + [pl.BlockSpec((tm, LANES), tab)] * 6,
        out_specs=[pl.BlockSpec((tm, w.shape[1]), row) for w in outs[:-1]]
        + [pl.BlockSpec((hq, tm), lambda i: (0, i))],
        out_shape=outs,
        compiler_params=_params(("arbitrary",)),
    )(x, g, w_ext, wuq, wuk, wvt, qn, kvn, *tabs)


def _ret_kernel(lg_ref, q_ref, k_ref, v_ref, g_ref, gn_ref, o_ref, s_ref, sb_ref, *, cs, nc):
    p = pl.program_id(1)
    c = pl.program_id(2)
    pos = lax.broadcasted_iota(I32, (cs, LANES), 0).astype(F32)

    @pl.when(c == 0)
    def _():
        s_ref[...] = jnp.zeros_like(s_ref)

    def decay(lg):
        return jnp.exp(jnp.full((1, LANES), cs, F32) * lg)

    def head(ref, h):
        return ref[:, h * LANES:(h + 1) * LANES]

    @pl.when(p == 0)
    def _():
        for h in range(RET_HEADS):
            lgb = lg_ref[1, h]
            sb_ref[nc - 1 - c, h] = s_ref[h]
            kw = (head(k_ref, h).astype(F32) * jnp.exp(pos * lgb)).astype(BF16)
            s_ref[h] = s_ref[h] * decay(lgb) + _dot_tn(kw, head(v_ref, h))

    @pl.when(p == 1)
    def _():
        ii = lax.broadcasted_iota(I32, (cs, cs), 0)
        jj = lax.broadcasted_iota(I32, (cs, cs), 1)
        dd = ii - jj
        dist = jnp.abs(dd).astype(F32)
        for h in range(RET_HEADS):
            lgf = lg_ref[0, h]
            lgb = lg_ref[1, h]
            q = head(q_ref, h)
            k = head(k_ref, h)
            v = head(v_ref, h)
            qf = q.astype(F32)
            intra = jnp.exp(dist * jnp.where(dd >= 0, lgf, lgb))
            o = _dot((_dot_nt(q, k) * intra).astype(BF16), v)
            qwf = (qf * jnp.exp((pos + 1.0) * lgf)).astype(BF16)
            qwb = (qf * jnp.exp((cs - pos) * lgb)).astype(BF16)
            o = o + _dot(qwf, s_ref[h].astype(BF16)) + _dot(qwb, sb_ref[c, h].astype(BF16))
            kw = (k.astype(F32) * jnp.exp((cs - 1.0 - pos) * lgf)).astype(BF16)
            s_ref[h] = s_ref[h] * decay(lgf) + _dot_tn(kw, v)
            o = o * lax.rsqrt(jnp.mean(o * o, axis=-1, keepdims=True) + EPS)
            g = head(g_ref, h).astype(F32)
            o_ref[:, h * LANES:(h + 1) * LANES] = (
                o * head(gn_ref, h) * (g * jax.nn.sigmoid(g))).astype(BF16)


def _retention(lg, q, k, v, g, gn, B, T, cs=256):
    n, w = q.shape
    nc = T // cs

    def chunk(p, c):
        return p * c + (1 - p) * (nc - 1 - c)

    blk = lambda b, p, c, lg: (b * nc + chunk(p, c), 0)
    return pl.pallas_call(
        functools.partial(_ret_kernel, cs=cs, nc=nc),
        name="retention",
        grid_spec=pltpu.PrefetchScalarGridSpec(
            num_scalar_prefetch=1,
            grid=(B, 2, nc),
            in_specs=[pl.BlockSpec((cs, w), blk)] * 4
            + [pl.BlockSpec((1, w), lambda b, p, c, lg: (0, 0))],
            out_specs=pl.BlockSpec((cs, w), lambda b, p, c, lg: (b * nc + p * c, 0)),
            scratch_shapes=[pltpu.VMEM((RET_HEADS, LANES, LANES), F32),
                            pltpu.VMEM((nc, RET_HEADS, LANES, LANES), F32)],
        ),
        out_shape=jax.ShapeDtypeStruct((n, w), BF16),
        compiler_params=_params(("arbitrary",) * 3),
    )(lg, q, k, v, g, gn)


def _mla_kernel(q_ref, k_ref, vt_ref, ot_ref, s_ref, p_ref, *, tk, nk, nchain):
    cols = q_ref.shape[0] // nchain
    qs = [q_ref[c * cols:(c + 1) * cols, :] for c in range(nchain)]

    def keys(j):
        return k_ref[pl.ds(pl.multiple_of(j * tk, tk), tk), :]

    def vals(j):
        return vt_ref[:, pl.ds(pl.multiple_of(j * tk, tk), tk)]

    def step(j, slot, carry):
        kn = keys(jnp.minimum(j + 1, nk - 1))
        vp = vals(jnp.maximum(j - 1, 0))
        out = []
        for c, (m, a_prev, acc) in enumerate(carry):
            s_ref[c, 1 - slot] = _dot_nt(kn, qs[c])
            acc = a_prev * acc + _dot(vp, p_ref[c, 1 - slot])
            s = s_ref[c, slot]
            m_new = jnp.maximum(m, jnp.max(s, axis=0, keepdims=True))
            p_ref[c, slot] = jnp.exp2(s - m_new).astype(BF16)
            out.append((m_new, jnp.exp2(m - m_new), acc))
        return tuple(out)

    k0 = keys(0)
    for c in range(nchain):
        s_ref[c, 0] = _dot_nt(k0, qs[c])
        p_ref[c, 1] = jnp.zeros(p_ref.shape[2:], BF16)
    init = tuple((jnp.full((1, cols), -jnp.inf, F32), jnp.ones((1, cols), F32),
                  jnp.zeros((LANES, cols), F32)) for _ in range(nchain))
    pairs = 2 if nk % 4 == 0 else 1

    def trip(t, cr):
        for u in range(pairs):
            j = 2 * (pairs * t + u)
            cr = step(j + 1, 1, step(j, 0, cr))
        return cr

    res = lax.fori_loop(0, nk // (2 * pairs), trip, init)
    vl = vals(nk - 1)
    for c, (_, a_prev, acc) in enumerate(res):
        acc = a_prev * acc + _dot(vl, p_ref[c, 1])
        ot_ref[:, c * cols:(c + 1) * cols] = (acc / acc[MLA_V:MLA_V + 1, :]).astype(BF16)


def _mla(q, k, vt, B, T, tq=512, tk=1024, nchain=2):
    n = q.shape[0]
    tq = min(tq, T)
    tk = min(tk, T // 2)
    nq = T // tq
    nk = T // tk
    assert nk % 2 == 0
    cols = tq // nchain
    return pl.pallas_call(
        functools.partial(_mla_kernel, tk=tk, nk=nk, nchain=nchain),
        name="mla",
        grid=(B, MLA_HEADS, nq),
        in_specs=[
            pl.BlockSpec((tq, LANES), lambda b, h, i: (b * nq + i, h)),
            pl.BlockSpec((T, LANES), lambda b, h, i: (b, h)),
            pl.BlockSpec((LANES, T), lambda b, h, i: (h, b)),
        ],
        out_specs=pl.BlockSpec((LANES, tq), lambda b, h, i: (h, b * nq + i)),
        out_shape=jax.ShapeDtypeStruct((MLA_HEADS * LANES, n), BF16),
        scratch_shapes=[pltpu.VMEM((nchain, 2, tk, cols), F32), pltpu.VMEM((nchain, 2, tk, cols), BF16)],
        compiler_params=_params(("arbitrary",) * 3),
    )(q, k, vt)


def _outproj_kernel(*refs, npairs, transposed):
    x_ref = refs[0]
    ys = refs[1:1 + npairs]
    ws = refs[1 + npairs:1 + 2 * npairs]
    g_ref, rt_ref, xo_ref, h_ref, aff_ref = refs[1 + 2 * npairs:]
    half = x_ref.shape[0] // 2
    for r in (slice(0, half), slice(half, 2 * half)):
        acc = x_ref[r, :]
        for y, w, t in zip(ys, ws, transposed):
            acc = acc + (_dot_tn(y[:, r], w[...]) if t else _dot(y[r, :], w[...]))
        xo_ref[r, :] = acc
        h = _rms(acc, g_ref[...])
        h_hi = h.astype(BF16)
        h_ref[r, :] = h_hi
        part = _dot_nt(rt_ref[...], h_hi)
        h_lo = (h - h_hi.astype(F32)).astype(BF16)
        logits = (part[0:N_EXPERTS] + part[N_EXPERTS:2 * N_EXPERTS]
                  + _dot_nt(rt_ref[0:N_EXPERTS, :], h_lo))
        e = jnp.exp(logits - jnp.max(logits, axis=0, keepdims=True))
        aff_ref[:, r] = e / jnp.sum(e, axis=0, keepdims=True)


def _outproj_router(x, ys, ws, g, router_t, transposed=None, tm=512):
    n, d = x.shape
    transposed = tuple(transposed or (False,) * len(ys))
    row = lambda i: (i, 0)
    const = lambda i: (0, 0)
    return pl.pallas_call(
        functools.partial(_outproj_kernel, npairs=len(ys), transposed=transposed),
        name="outproj_router",
        grid=(n // tm,),
        in_specs=[pl.BlockSpec((tm, d), row)]
        + [pl.BlockSpec((y.shape[0], tm), lambda i: (0, i)) if t
           else pl.BlockSpec((tm, y.shape[1]), row) for y, t in zip(ys, transposed)]
        + [pl.BlockSpec(w.shape, const) for w in ws]
        + [pl.BlockSpec((1, d), const), pl.BlockSpec((2 * N_EXPERTS, d), const)],
        out_specs=[pl.BlockSpec((tm, d), row), pl.BlockSpec((tm, d), row),
                   pl.BlockSpec((N_EXPERTS, tm), lambda i: (0, i))],
        out_shape=[jax.ShapeDtypeStruct((n, d), F32), jax.ShapeDtypeStruct((n, d), BF16),
                   jax.ShapeDtypeStruct((N_EXPERTS, n), F32)],
        compiler_params=_params(("arbitrary",)),
    )(x, *ys, *ws, g, router_t)


def _rg_in_kernel(x_ref, g_ref, w_ref, gate_out, xr_out):
    hb = _rms(x_ref[...], g_ref[...]).astype(BF16)
    gate_out[...] = _dot(hb, w_ref[:, 0:D_RNN]).astype(BF16)
    xr_out[...] = _dot(hb, w_ref[:, D_RNN:2 * D_RNN])


def _rg_in(x, g, w, tm=256):
    n, d = x.shape
    row = lambda i: (i, 0)
    const = lambda i: (0, 0)
    return pl.pallas_call(
        _rg_in_kernel,
        name="rg_in",
        grid=(n // tm,),
        in_specs=[pl.BlockSpec((tm, d), row), pl.BlockSpec((1, d), const),
                  pl.BlockSpec((d, 2 * D_RNN), const)],
        out_specs=[pl.BlockSpec((tm, D_RNN), row), pl.BlockSpec((tm, D_RNN), row)],
        out_shape=[jax.ShapeDtypeStruct((n, D_RNN), BF16), jax.ShapeDtypeStruct((n, D_RNN), F32)],
        compiler_params=_params(("arbitrary",)),
    )(x, g, w)


HALO = 8
RG_LANES = 512
SUBLANES = 8


def _rg_scan_kernel(xr_ref, prev_ref, next_ref, gate_ref, cw_ref, cb_ref, wa_ref, ba_ref, wx_ref,
                    bx_ref, lam_ref, y_ref, carry_ref, hb_ref, *, tc, nt):
    p = pl.program_id(2)
    c = pl.program_id(3)
    chunk = p * c + (1 - p) * (nt - 1 - c)

    @pl.when(c == 0)
    def _():
        carry_ref[...] = jnp.zeros_like(carry_ref)

    cur = xr_ref[...]
    prev = jnp.where(chunk > 0, prev_ref[...], 0.0)
    nxt = jnp.where(chunk < nt - 1, next_ref[...], 0.0)
    ext = jnp.concatenate([prev, cur, nxt], axis=0)
    xc = cb_ref[...]
    for j in range(CONV_W):
        o = HALO - CONV_LEFT + j
        xc = xc + ext[o:o + tc, :] * cw_ref[j:j + 1, :]
    xcb = xc.astype(BF16)

    def block_diag(w_ref):
        return jnp.concatenate(
            [_dot(xcb[:, k * RG_BLK:(k + 1) * RG_BLK], w_ref[0, k])
             for k in range(RG_LANES // RG_BLK)], axis=1)

    r = jax.nn.sigmoid(block_diag(wa_ref) + ba_ref[0])
    gi = jax.nn.sigmoid(block_diag(wx_ref) + bx_ref[0])
    nl = -lam_ref[0]
    softplus = jnp.maximum(nl, 0.0) + jnp.log1p(jnp.exp(-jnp.abs(nl)))
    log_a = -RG_C * r * softplus
    a = jnp.exp(log_a)
    om = 1.0 - a * a
    b = jnp.where(om > 0.0, om * lax.rsqrt(om), 0.0) * (gi * xc)
    sub = lax.broadcasted_iota(I32, (tc, RG_LANES), 0) % SUBLANES

    def scan(a, b, reverse):
        groups = tc // SUBLANES
        shape3 = (groups, SUBLANES, RG_LANES)
        a = a.reshape(shape3)
        b = b.reshape(shape3)
        sub3 = sub.reshape(shape3)
        for s in (1, 2, 4):
            keep = (sub3 < SUBLANES - s) if reverse else (sub3 >= s)
            sh = SUBLANES - s if reverse else s
            a_sh = jnp.where(keep, pltpu.roll(a, sh, 1), 1.0)
            b_sh = jnp.where(keep, pltpu.roll(b, sh, 1), 0.0)
            b = a * b_sh + b
            a = a * a_sh
        a = a.reshape(tc, RG_LANES)
        b = b.reshape(tc, RG_LANES)
        out = [None] * groups
        carry = carry_ref[0:1, :]
        for g in (range(groups - 1, -1, -1) if reverse else range(groups)):
            rows = slice(g * SUBLANES, (g + 1) * SUBLANES)
            out[g] = a[rows] * carry + b[rows]
            carry = out[g][0:1] if reverse else out[g][SUBLANES - 1:SUBLANES]
        carry_ref[...] = jnp.broadcast_to(carry, carry_ref.shape)
        return jnp.concatenate(out, axis=0)

    @pl.when(p == 0)
    def _():
        hb_ref[chunk] = scan(a, b, True).astype(BF16)

    @pl.when(p == 1)
    def _():
        hs = scan(a, b, False) + hb_ref[chunk].astype(F32)
        y_ref[...] = (jax.nn.gelu(gate_ref[...].astype(F32)) * hs).astype(BF16)


def _rg_scan(xr, gate, cw, cb, wa, ba, wx, bx, lam, B, T, tc=256):
    n = xr.shape[0]
    tc = min(tc, T)
    nt = T // tc
    hpc = tc // HALO
    gb = RG_LANES // RG_BLK

    def chunk(p, c):
        return p * c + (1 - p) * (nt - 1 - c)

    cur = lambda b, j, p, c: (b * nt + chunk(p, c), j)
    prv = lambda b, j, p, c: (jnp.maximum((b * nt + chunk(p, c)) * hpc - 1, 0), j)
    nxt = lambda b, j, p, c: (jnp.minimum((b * nt + chunk(p, c) + 1) * hpc, n // HALO - 1), j)
    par = lambda b, j, p, c: (1 - p, j, 0, 0)
    vec = lambda b, j, p, c: (1 - p, 0, j)
    return pl.pallas_call(
        functools.partial(_rg_scan_kernel, tc=tc, nt=nt),
        name="rg_scan",
        grid=(B, D_RNN // RG_LANES, 2, nt),
        in_specs=[
            pl.BlockSpec((tc, RG_LANES), cur),
            pl.BlockSpec((HALO, RG_LANES), prv),
            pl.BlockSpec((HALO, RG_LANES), nxt),
            pl.BlockSpec((tc, RG_LANES), cur),
            pl.BlockSpec((CONV_W, RG_LANES), lambda b, j, p, c: (0, j)),
            pl.BlockSpec((1, RG_LANES), lambda b, j, p, c: (0, j)),
            pl.BlockSpec((1, gb, RG_BLK, RG_BLK), par),
            pl.BlockSpec((1, 1, RG_LANES), vec),
            pl.BlockSpec((1, gb, RG_BLK, RG_BLK), par),
            pl.BlockSpec((1, 1, RG_LANES), vec),
            pl.BlockSpec((1, 1, RG_LANES), vec),
        ],
        out_specs=pl.BlockSpec((tc, RG_LANES), lambda b, j, p, c: (b * nt + p * c, j)),
        out_shape=jax.ShapeDtypeStruct((n, D_RNN), BF16),
        scratch_shapes=[pltpu.VMEM((HALO, RG_LANES), F32), pltpu.VMEM((nt, tc, RG_LANES), BF16)],
        compiler_params=_params(("arbitrary",) * 4),
    )(xr, xr, xr, gate, cw, cb, wa, ba, wx, bx, lam)


SEL_BLK = 256


def _select_kernel(aff_ref, pos_ref, gate_ref, cs_ref, *, cap, nblk):
    keys = pltpu.bitcast(aff_ref[...], I32)

    def search(i, thr):
        cand = thr | lax.shift_left(jnp.int32(1), 30 - i)
        cnt = jnp.sum(jnp.where(keys >= cand, 1.0, 0.0), axis=1, keepdims=True)
        return jnp.where(cnt >= cap, cand, thr)

    thr = lax.fori_loop(0, 31, search, jnp.zeros((N_EXPERTS, 1), I32))
    need = cap - jnp.sum(jnp.where(keys > thr, 1.0, 0.0), axis=1, keepdims=True)
    ri = lax.broadcasted_iota(I32, (SEL_BLK, SEL_BLK), 0)
    ci = lax.broadcasted_iota(I32, (SEL_BLK, SEL_BLK), 1)
    tri = jnp.where(ri <= ci, 1.0, 0.0).astype(BF16)

    def body(j, carry):
        ceq, csel = carry
        off = pl.multiple_of(j * SEL_BLK, SEL_BLK)
        a = aff_ref[:, pl.ds(off, SEL_BLK)]
        kk = pltpu.bitcast(a, I32)
        gt = kk > thr
        eq = kk == thr
        eqf = jnp.where(eq, 1.0, 0.0)
        eqc = _dot(eqf.astype(BF16), tri) + ceq
        sel = gt | (eq & (eqc <= need))
        self_ = jnp.where(sel, 1.0, 0.0)
        selc = _dot(self_.astype(BF16), tri) + csel
        pos_ref[:, pl.ds(off, SEL_BLK)] = jnp.where(sel, selc - 1.0, -1.0).astype(I32)
        gate_ref[:, pl.ds(off, SEL_BLK)] = jnp.where(sel, a, 0.0)
        cs_ref[j] = jnp.broadcast_to(csel, (N_EXPERTS, LANES)).astype(I32)
        return (ceq + jnp.sum(eqf, axis=1, keepdims=True),
                csel + jnp.sum(self_, axis=1, keepdims=True))

    z = jnp.zeros((N_EXPERTS, 1), F32)
    lax.fori_loop(0, nblk, body, (z, z))


def _select(aff_t, cap):
    e, n = aff_t.shape
    nblk = n // SEL_BLK
    return pl.pallas_call(
        functools.partial(_select_kernel, cap=cap, nblk=nblk),
        name="moe_select",
        out_shape=[jax.ShapeDtypeStruct((e, n), I32), jax.ShapeDtypeStruct((e, n), F32),
                   jax.ShapeDtypeStruct((nblk, e, LANES), I32)],
        compiler_params=pltpu.CompilerParams(vmem_limit_bytes=VMEM_LIMIT),
    )(aff_t)


def _window(cs_ref, e, i):
    c = cs_ref[e, i]
    c1 = cs_ref[e, i + 1]
    shift = SLOT_ALIGN.bit_length() - 1
    b0 = lax.shift_left(lax.shift_right_logical(c, shift), shift)
    npass = lax.shift_right_logical(c1 - b0 + (SLOT_WIN - 1), SLOT_WIN.bit_length() - 1)
    return b0, npass


def _compact_kernel(cs_ref, h_ref, pos_ref, xg_ref, *, cap, tiles):
    e = pl.program_id(0)
    i = pl.program_id(1)

    @pl.when(i == 0)
    def _():
        xg_ref[...] = jnp.zeros_like(xg_ref)

    rows = lax.broadcasted_iota(I32, (SLOT_WIN, SEL_BLK), 0)

    def place(u, b0, k):
        lo = b0 + k * SLOT_WIN
        base = pl.multiple_of(jnp.minimum(lo, cap - SLOT_WIN), SLOT_ALIGN)
        pos = pos_ref[0, :, u * SEL_BLK:(u + 1) * SEL_BLK]
        tgt = jnp.where((pos >= lo) & (pos < lo + SLOT_WIN), pos - base, -1)
        onehot = jnp.where(rows == tgt, 1.0, 0.0).astype(BF16)
        x = h_ref[u * SEL_BLK:(u + 1) * SEL_BLK, :]
        xg_ref[0, pl.ds(base, SLOT_WIN), :] += _dot(onehot, x).astype(BF16)

    wins = [_window(cs_ref, e, i * tiles + u) for u in range(tiles)]
    for u, (b0, _) in enumerate(wins):
        place(u, b0, 0)
    for u, (b0, npass) in enumerate(wins):
        lax.fori_loop(1, npass, lambda k, c, u=u, b0=b0: (place(u, b0, k), c)[1], 0)


def _compact(cs, h, pos3, cap, tiles=8):
    n, d = h.shape
    tiles = min(tiles, n // SEL_BLK)
    blk = tiles * SEL_BLK
    return pl.pallas_call(
        functools.partial(_compact_kernel, cap=cap, tiles=tiles),
        name="moe_compact",
        grid_spec=pltpu.PrefetchScalarGridSpec(
            num_scalar_prefetch=1,
            grid=(N_EXPERTS, n // blk),
            in_specs=[pl.BlockSpec((blk, d), lambda e, i, cs: (i, 0)),
                      pl.BlockSpec((1, 1, blk), lambda e, i, cs: (e, 0, i))],
            out_specs=pl.BlockSpec((1, cap, d), lambda e, i, cs: (e, 0, 0)),
        ),
        out_shape=jax.ShapeDtypeStruct((N_EXPERTS, cap, d), BF16),
        compiler_params=_params(("arbitrary",) * 2),
    )(cs, h, pos3)


def _ffn_kernel(x_ref, wg_ref, wu_ref, wd_ref, o_ref, acc_ref):
    f = pl.program_id(2)

    @pl.when(f == 0)
    def _():
        acc_ref[...] = jnp.zeros_like(acc_ref)

    x = x_ref[0]
    a = _dot(x, wg_ref[0, 0].astype(BF16))
    hid = (a * jax.nn.sigmoid(a)) * _dot(x, wu_ref[0, 0].astype(BF16))
    acc_ref[...] += _dot(hid.astype(BF16), wd_ref[0, 0].astype(BF16))

    @pl.when(f == pl.num_programs(2) - 1)
    def _():
        o_ref[0] = acc_ref[...].astype(BF16)


def _ffn(xg, wg, wu, wd, layer, tm=1024, tf=1024):
    e, cap, d = xg.shape
    fdim = wg.shape[3]
    tm = min(tm, cap)
    return pl.pallas_call(
        _ffn_kernel,
        name="moe_ffn",
        grid=(e, cap // tm, fdim // tf),
        in_specs=[pl.BlockSpec((1, tm, d), lambda e, m, f: (e, m, 0)),
                  pl.BlockSpec((1, 1, d, tf), lambda e, m, f: (layer, e, 0, f)),
                  pl.BlockSpec((1, 1, d, tf), lambda e, m, f: (layer, e, 0, f)),
                  pl.BlockSpec((1, 1, tf, d), lambda e, m, f: (layer, e, f, 0))],
        out_specs=pl.BlockSpec((1, tm, d), lambda e, m, f: (e, m, 0)),
        out_shape=jax.ShapeDtypeStruct((e, cap, d), BF16),
        scratch_shapes=[pltpu.VMEM((tm, d), F32)],
        compiler_params=_params(("arbitrary",) * 3),
    )(xg, wg, wu, wd)


def _combine_kernel(cs_ref, x_ref, gate_ref, pos_ref, om_ref, xo_ref, buf, xbuf, sem, xsem,
                    *, cap, nt):
    i = pl.program_id(0)

    def win_copy(e, t, slot):
        b0, _ = _window(cs_ref, e, t)
        base = pl.multiple_of(jnp.minimum(b0, cap - SLOT_WIN), SLOT_ALIGN)
        return pltpu.make_async_copy(om_ref.at[e, pl.ds(base, SLOT_WIN), :], buf.at[slot, e],
                                     sem.at[slot, e])

    @pl.when(i == 0)
    def _():
        for e in range(N_EXPERTS):
            win_copy(e, 0, 0).start()

    @pl.when(i + 1 < nt)
    def _():
        for e in range(N_EXPERTS):
            win_copy(e, i + 1, (i + 1) % 2).start()

    slot = i % 2
    lane = lax.broadcasted_iota(I32, (SEL_BLK, SLOT_WIN), 1)

    ecol = lax.broadcasted_iota(I32, (1, N_EXPERTS), 1)
    lo_row = jnp.zeros((1, N_EXPERTS), I32)
    for e in range(N_EXPERTS):
        win_copy(e, i, slot).wait()
        lo_row = jnp.where(ecol == e, _window(cs_ref, e, i)[0], lo_row)
    pos_all = pos_ref[...]
    gate_all = gate_ref[...]
    tgt_all = jnp.where((pos_all >= lo_row) & (pos_all < lo_row + SLOT_WIN),
                        pos_all - jnp.minimum(lo_row, cap - SLOT_WIN), -1)
    onehot = jnp.concatenate(
        [jnp.where(lane == tgt_all[:, e:e + 1], gate_all[:, e:e + 1], 0.0).astype(BF16)
         for e in range(N_EXPERTS)], axis=1)
    rows = buf[slot].reshape(N_EXPERTS * SLOT_WIN, buf.shape[-1])
    xo_ref[...] = x_ref[...] + _dot(onehot, rows)

    for e in range(N_EXPERTS):
        b0, npass = _window(cs_ref, e, i)

        def extra(k, _, e=e, b0=b0):
            lo = b0 + k * SLOT_WIN
            base = pl.multiple_of(jnp.minimum(lo, cap - SLOT_WIN), SLOT_ALIGN)
            cp = pltpu.make_async_copy(om_ref.at[e, pl.ds(base, SLOT_WIN), :], xbuf, xsem)
            cp.start()
            cp.wait()
            pcol = pos_ref[:, e:e + 1]
            tgt = jnp.where((pcol >= lo) & (pcol < lo + SLOT_WIN), pcol - base, -1)
            onehot = jnp.where(lane == tgt, gate_ref[:, e:e + 1], 0.0).astype(BF16)
            xo_ref[...] += _dot(onehot, xbuf[...])
            return 0

        lax.fori_loop(1, npass, extra, 0)


def _combine(cs, x, gate_t, pos_t, om, cap):
    n, d = x.shape
    nt = n // SEL_BLK
    row = lambda i, cs: (i, 0)
    return pl.pallas_call(
        functools.partial(_combine_kernel, cap=cap, nt=nt),
        name="moe_combine",
        grid_spec=pltpu.PrefetchScalarGridSpec(
            num_scalar_prefetch=1,
            grid=(nt,),
            in_specs=[pl.BlockSpec((SEL_BLK, d), row),
                      pl.BlockSpec((SEL_BLK, N_EXPERTS), row),
                      pl.BlockSpec((SEL_BLK, N_EXPERTS), row),
                      pl.BlockSpec(memory_space=pl.ANY)],
            out_specs=pl.BlockSpec((SEL_BLK, d), row),
            scratch_shapes=[pltpu.VMEM((2, N_EXPERTS, SLOT_WIN, d), BF16),
                            pltpu.VMEM((SLOT_WIN, d), BF16),
                            pltpu.SemaphoreType.DMA((2, N_EXPERTS)),
                            pltpu.SemaphoreType.DMA(())],
        ),
        out_shape=jax.ShapeDtypeStruct((n, d), F32),
        compiler_params=_params(("arbitrary",)),
    )(cs, x, gate_t, pos_t, om)


def _moe(x, h, aff_t, wg, wu, wd, layer):
    n = x.shape[0]
    cap = max(1, EC_FACTOR * n // N_EXPERTS)
    pos, gate, cs3 = _select(aff_t, cap)
    cs = jnp.concatenate([cs3[:, :, 0].T, jnp.full((N_EXPERTS, 1), cap, I32)], axis=1)
    xg = _compact(cs, h, pos.reshape(N_EXPERTS, 1, n), cap)
    om = _ffn(xg, wg, wu, wd, layer)
    return _combine(cs, x, gate.T, pos.T, om, cap)


def _final_kernel(x_ref, g_ref, o_ref):
    o_ref[...] = _rms(x_ref[...], g_ref[...])


def _final_norm(x, g, tm=512):
    n, d = x.shape
    return pl.pallas_call(
        _final_kernel,
        name="final_norm",
        grid=(n // tm,),
        in_specs=[pl.BlockSpec((tm, d), lambda i: (i, 0)), pl.BlockSpec((1, d), lambda i: (0, 0))],
        out_specs=pl.BlockSpec((tm, d), lambda i: (i, 0)),
        out_shape=jax.ShapeDtypeStruct((n, d), F32),
        compiler_params=_params(("arbitrary",)),
    )(x, g)


def _rot_half_cols(w):
    half = w.shape[-1] // 2
    return jnp.concatenate([-w[..., half:], w[..., :half]], axis=-1)


def _pad_heads(w, heads, dim, lead=0):
    d = w.shape[0]
    w = w.reshape(d, heads, dim)
    w = jnp.pad(w, ((0, 0), (0, 0), (lead, LANES - lead - dim)))
    return w.reshape(d, heads * LANES)


def _prep_ab(w_in, w_uq, w_ukv, w_out):
    d = w_in.shape[0]
    cuts = [0, 256, 512, 1024, 1536, 1792, 1920, 1952]
    q_r, k_r, v_r, g_r, c_q, c_kv, k_pe = [w_in[:, a:b] for a, b in zip(cuts[:-1], cuts[1:])]
    q4 = q_r.reshape(d, RET_HEADS, RET_DK)
    k4 = k_r.reshape(d, RET_HEADS, RET_DK) * (RET_DK ** -0.5)
    ph = lambda w: _pad_heads(w.reshape(d, -1), RET_HEADS, RET_DK)
    kpe = jnp.pad(k_pe, ((0, 0), (MLA_NOPE, LANES - MLA_NOPE - MLA_ROPE)))
    kpe_rot = jnp.pad(_rot_half_cols(k_pe), ((0, 0), (MLA_NOPE, LANES - MLA_NOPE - MLA_ROPE)))
    w_ext = jnp.concatenate([ph(q4), ph(_rot_half_cols(q4)), ph(k4), ph(_rot_half_cols(k4)),
                             v_r, g_r, c_q, c_kv, kpe, kpe_rot], axis=1).astype(BF16)
    uq = w_uq.reshape(MLA_Q_LORA, MLA_HEADS, MLA_NOPE + MLA_ROPE)
    qa = jnp.pad(uq, ((0, 0), (0, 0), (0, LANES - MLA_NOPE - MLA_ROPE)))
    qb = jnp.pad(_rot_half_cols(uq[..., MLA_NOPE:]),
                 ((0, 0), (0, 0), (MLA_NOPE, LANES - MLA_NOPE - MLA_ROPE)))
    wuq = jnp.concatenate([qa.reshape(MLA_Q_LORA, -1), qb.reshape(MLA_Q_LORA, -1)], axis=1)
    ukv = w_ukv.reshape(MLA_KV_LORA, MLA_HEADS, MLA_NOPE + MLA_V)
    kn = jnp.pad(ukv[..., :MLA_NOPE], ((0, 0), (0, 0), (0, LANES - MLA_NOPE)))
    vv = jnp.pad(ukv[..., MLA_NOPE:], ((0, 0), (0, 0), (0, LANES - MLA_V)))
    wuk = kn.reshape(MLA_KV_LORA, -1)
    wvt = vv.reshape(MLA_KV_LORA, -1).T
    wo_a = w_out[:RET_HEADS * RET_DV]
    wo_b = w_out[RET_HEADS * RET_DV:].reshape(MLA_HEADS, MLA_V, d)
    wo_b = jnp.pad(wo_b, ((0, 0), (0, LANES - MLA_V), (0, 0))).reshape(MLA_HEADS * LANES, d)
    return (w_ext, wuq.astype(BF16), wuk.astype(BF16), wvt.astype(BF16), wo_a.astype(BF16),
            wo_b.astype(BF16))


def _rope_tabs(T):
    def tables(dim):
        inv = 1.0 / (ROPE_BASE ** (jnp.arange(0, dim, 2, dtype=F32) / dim))
        ang = jnp.arange(T, dtype=F32)[:, None] * inv[None, :]
        return jnp.cos(ang), jnp.sin(ang)

    cr, sr = tables(RET_DK)
    cr = jnp.tile(cr, (1, LANES // (RET_DK // 2)))
    sr = jnp.tile(sr, (1, LANES // (RET_DK // 2)))
    cm, sm = tables(MLA_ROPE)
    scale = (MLA_NOPE + MLA_ROPE) ** -0.5 * math.log2(math.e)
    ones = jnp.ones((T, MLA_NOPE), F32)
    z_lo = jnp.zeros((T, MLA_NOPE), F32)
    z_hi = jnp.zeros((T, LANES - MLA_NOPE - MLA_ROPE), F32)
    cq = jnp.concatenate([ones, cm, cm, z_hi], axis=1) * scale
    sq = jnp.concatenate([z_lo, sm, sm, z_hi], axis=1) * scale
    ck = jnp.concatenate([z_lo, cm, cm, z_hi], axis=1)
    sk = jnp.concatenate([z_lo, sm, sm, z_hi], axis=1)
    return cr, sr, cq, sq, ck, sk


def _trunk(x3, p):
    B, T, d = x3.shape
    x = x3.reshape(B * T, d)
    tabs = _rope_tabs(T)
    for l in range(DEPTH):
        j = l // 2
        g_mix = p["norm_mix"][l].reshape(1, d)
        g_ffn = p["norm_ffn"][l].reshape(1, d)
        rt = p["moe_router"][l].T
        rt_hi = rt.astype(BF16)
        router_t = jnp.concatenate([rt_hi, (rt - rt_hi.astype(F32)).astype(BF16)], axis=0)
        if l % 2 == 0:
            w_ext, wuq, wuk, wvt, wo_a, wo_b = p["ab"][j]
            qr, kr, vr, gr, q, k, vt = _ab_in(x, g_mix, w_ext, wuq, wuk, wvt,
                                             p["mla_q_norm"][j].reshape(1, -1),
                                             p["mla_kv_norm"][j].reshape(1, -1), tabs, T)
            lg = jax.nn.log_sigmoid(p["ret_decay_logit"][j].astype(F32))
            y_a = _retention(lg, qr, kr, vr, gr, p["ret_gn"][j].reshape(1, -1), B, T)
            y_bt = _mla(q, k, vt, B, T)
            x, h, aff_t = _outproj_router(x, [y_a, y_bt], [wo_a, wo_b], g_ffn, router_t,
                                          transposed=(False, True))
        else:
            gate, xr = _rg_in(x, g_mix, p["rg_w_in"][j])
            y = _rg_scan(xr, gate, p["rg_conv_w"][j], p["rg_conv_b"][j].reshape(1, -1),
                         p["rg_wa"][j], p["rg_ba"][j].reshape(2, 1, -1), p["rg_wx"][j],
                         p["rg_bx"][j].reshape(2, 1, -1), p["rg_lambda"][j].reshape(2, 1, -1), B, T)
            x, h, aff_t = _outproj_router(x, [y], [p["rg_w_out"][j]], g_ffn, router_t)
        x = _moe(x, h, aff_t, p["moe_w_gate"], p["moe_w_up"], p["moe_w_down"], l)
    return _final_norm(x, p["norm_final"].reshape(1, d)).reshape(B, T, d)


def kernel(x_prompt, x_sample, norm_mix, norm_ffn, norm_final, ab_w_in, ret_decay_logit, ret_gn, mla_q_norm, mla_w_uq, mla_kv_norm, mla_w_ukv, ab_w_out, rg_w_in, rg_conv_w, rg_conv_b, rg_wa, rg_ba, rg_wx, rg_bx, rg_lambda, rg_w_out, moe_router, moe_w_gate, moe_w_up, moe_w_down):
    p = dict(
        norm_mix=norm_mix, norm_ffn=norm_ffn, norm_final=norm_final,
        ret_decay_logit=ret_decay_logit, ret_gn=ret_gn, mla_q_norm=mla_q_norm,
        mla_kv_norm=mla_kv_norm, rg_conv_w=rg_conv_w, rg_conv_b=rg_conv_b, rg_ba=rg_ba,
        rg_bx=rg_bx, rg_lambda=rg_lambda, moe_router=moe_router,
        ab=[_prep_ab(ab_w_in[j], mla_w_uq[j], mla_w_ukv[j], ab_w_out[j])
            for j in range(ab_w_in.shape[0])],
        rg_w_in=rg_w_in.astype(BF16), rg_wa=rg_wa.astype(BF16), rg_wx=rg_wx.astype(BF16),
        rg_w_out=rg_w_out.astype(BF16), moe_w_gate=moe_w_gate, moe_w_up=moe_w_up,
        moe_w_down=moe_w_down,
    )
    return _trunk(x_prompt, p), _trunk(x_sample, p)
```

```python
import functools
import math

import jax
import jax.numpy as jnp
from jax import lax
from jax.experimental import pallas as pl
from jax.experimental.pallas import tpu as pltpu

F32 = jnp.float32
BF16 = jnp.bfloat16
I32 = jnp.int32

D_MODEL = 1024
DEPTH = 4
EPS = 1e-6
ROPE_BASE = 10000.0
RET_HEADS = 4
RET_DK = 64
RET_DV = 128
MLA_HEADS = 8
MLA_NOPE = 64
MLA_ROPE = 32
MLA_V = 64
MLA_Q_LORA = 256
MLA_KV_LORA = 128
D_RNN = D_MODEL
RG_BLOCKS = 8
RG_BLK = D_RNN // RG_BLOCKS
CONV_W = 4
CONV_LEFT = CONV_W // 2
RG_C = 8.0
N_EXPERTS = 16
EC_FACTOR = 2
D_EXPERT = 2 * D_MODEL

LANES = 128
SLOT_ALIGN = 64
SLOT_WIN = 128
VMEM_LIMIT = 56 * 1024 * 1024


def _params(sem):
    return pltpu.CompilerParams(dimension_semantics=sem, vmem_limit_bytes=VMEM_LIMIT)


def _rms(x, g):
    return x * lax.rsqrt(jnp.mean(x * x, axis=-1, keepdims=True) + EPS) * g


def _dot(a, b):
    return jnp.dot(a, b, preferred_element_type=F32)


def _dot_nt(a, b):
    return lax.dot_general(a, b, (((1,), (1,)), ((), ())), preferred_element_type=F32)


def _dot_tn(a, b):
    return lax.dot_general(a, b, (((0,), (0,)), ((), ())), preferred_element_type=F32)


AB_SEG = {
    "q": (0, 512), "q_rot": (512, 1024), "k": (1024, 1536), "k_rot": (1536, 2048),
    "v": (2048, 2560), "g": (2560, 3072), "c_q": (3072, 3328), "c_kv": (3328, 3456),
    "k_pe": (3456, 3584), "k_pe_rot": (3584, 3712),
}
AB_EXT = 3712


def _ab_in_kernel(x_ref, g_ref, w_ref, wuq_ref, wuk_ref, wvt_ref, qn_ref, kvn_ref, cr_ref, sr_ref,
                  cq_ref, sq_ref, ck_ref, sk_ref,
                  qr_out, kr_out, vr_out, gr_out, q_out, k_out, vt_out):
    hb = _rms(x_ref[...], g_ref[...]).astype(BF16)

    def seg(name):
        a, b = AB_SEG[name]
        return _dot(hb, w_ref[:, a:b])

    cr = jnp.tile(cr_ref[...], (1, RET_HEADS))
    sr = jnp.tile(sr_ref[...], (1, RET_HEADS))
    qr_out[...] = (seg("q") * cr + seg("q_rot") * sr).astype(BF16)
    kr_out[...] = (seg("k") * cr + seg("k_rot") * sr).astype(BF16)
    vr_out[...] = seg("v").astype(BF16)
    gr_out[...] = seg("g").astype(BF16)

    cqn = _rms(seg("c_q"), qn_ref[...]).astype(BF16)
    hq = MLA_HEADS * LANES
    qa = _dot(cqn, wuq_ref[:, 0:hq])
    qb = _dot(cqn, wuq_ref[:, hq:2 * hq])
    q_out[...] = (qa * jnp.tile(cq_ref[...], (1, MLA_HEADS))
                  + qb * jnp.tile(sq_ref[...], (1, MLA_HEADS))).astype(BF16)

    ckvn = _rms(seg("c_kv"), kvn_ref[...]).astype(BF16)
    kn = _dot(ckvn, wuk_ref[...])
    kpe = seg("k_pe") * ck_ref[...] + seg("k_pe_rot") * sk_ref[...]
    k_out[...] = (kn + jnp.tile(kpe, (1, MLA_HEADS))).astype(BF16)
    vt = _dot_nt(wvt_ref[...], ckvn)
    row = lax.broadcasted_iota(I32, vt.shape, 0)
    vt_out[...] = jnp.where(row % LANES == MLA_V, 1.0, vt).astype(BF16)


def _ab_in(x, g, w_ext, wuq, wuk, wvt, qn, kvn, tabs, T, tm=256):
    n, d = x.shape
    nt = T // tm
    row = lambda i: (i, 0)
    const = lambda i: (0, 0)
    tab = lambda i: (i % nt, 0)
    hq = MLA_HEADS * LANES
    hr = RET_HEADS * LANES
    outs = [jax.ShapeDtypeStruct((n, w), BF16) for w in (hr, hr, hr, hr, hq, hq)]
    outs.append(jax.ShapeDtypeStruct((hq, n), BF16))
    return pl.pallas_call(
        _ab_in_kernel,
        name="ab_in",
        grid=(n // tm,),
        in_specs=[
            pl.BlockSpec((tm, d), row),
            pl.BlockSpec((1, d), const),
            pl.BlockSpec((d, AB_EXT), const),
            pl.BlockSpec((MLA_Q_LORA, 2 * hq), const),
            pl.BlockSpec((MLA_KV_LORA, hq), const),
            pl.BlockSpec((hq, MLA_KV_LORA), const),
            pl.BlockSpec((1, MLA_Q_LORA), const),
            pl.BlockSpec((1, MLA_KV_LORA), const),
        ] + [pl.BlockSpec((tm, LANES), tab)] * 6,
        out_specs=[pl.BlockSpec((tm, w.shape[1]), row) for w in outs[:-1]]
        + [pl.BlockSpec((hq, tm), lambda i: (0, i))],
        out_shape=outs,
        compiler_params=_params(("arbitrary",)),
    )(x, g, w_ext, wuq, wuk, wvt, qn, kvn, *tabs)


def _ret_kernel(lg_ref, q_ref, k_ref, v_ref, g_ref, gn_ref, o_ref, s_ref, sb_ref, *, cs, nc):
    p = pl.program_id(1)
    c = pl.program_id(2)
    pos = lax.broadcasted_iota(I32, (cs, LANES), 0).astype(F32)

    @pl.when(c == 0)
    def _():
        s_ref[...] = jnp.zeros_like(s_ref)

    def decay(lg):
        return jnp.exp(jnp.full((1, LANES), cs, F32) * lg)

    def head(ref, h):
        return ref[:, h * LANES:(h + 1) * LANES]

    @pl.when(p == 0)
    def _():
        for h in range(RET_HEADS):
            lgb = lg_ref[1, h]
            sb_ref[nc - 1 - c, h] = s_ref[h]
            kw = (head(k_ref, h).astype(F32) * jnp.exp(pos * lgb)).astype(BF16)
            s_ref[h] = s_ref[h] * decay(lgb) + _dot_tn(kw, head(v_ref, h))

    @pl.when(p == 1)
    def _():
        ii = lax.broadcasted_iota(I32, (cs, cs), 0)
        jj = lax.broadcasted_iota(I32, (cs, cs), 1)
        dd = ii - jj
        dist = jnp.abs(dd).astype(F32)
        for h in range(RET_HEADS):
            lgf = lg_ref[0, h]
            lgb = lg_ref[1, h]
            q = head(q_ref, h)
            k = head(k_ref, h)
            v = head(v_ref, h)
            qf = q.astype(F32)
            intra = jnp.exp(dist * jnp.where(dd >= 0, lgf, lgb))
            o = _dot((_dot_nt(q, k) * intra).astype(BF16), v)
            qwf = (qf * jnp.exp((pos + 1.0) * lgf)).astype(BF16)
            qwb = (qf * jnp.exp((cs - pos) * lgb)).astype(BF16)
            o = o + _dot(qwf, s_ref[h].astype(BF16)) + _dot(qwb, sb_ref[c, h].astype(BF16))
            kw = (k.astype(F32) * jnp.exp((cs - 1.0 - pos) * lgf)).astype(BF16)
            s_ref[h] = s_ref[h] * decay(lgf) + _dot_tn(kw, v)
            o = o * lax.rsqrt(jnp.mean(o * o, axis=-1, keepdims=True) + EPS)
            g = head(g_ref, h).astype(F32)
            o_ref[:, h * LANES:(h + 1) * LANES] = (
                o * head(gn_ref, h) * (g * jax.nn.sigmoid(g))).astype(BF16)


def _retention(lg, q, k, v, g, gn, B, T, cs=256):
    n, w = q.shape
    nc = T // cs

    def chunk(p, c):
        return p * c + (1 - p) * (nc - 1 - c)

    blk = lambda b, p, c, lg: (b * nc + chunk(p, c), 0)
    return pl.pallas_call(
        functools.partial(_ret_kernel, cs=cs, nc=nc),
        name="retention",
        grid_spec=pltpu.PrefetchScalarGridSpec(
            num_scalar_prefetch=1,
            grid=(B, 2, nc),
            in_specs=[pl.BlockSpec((cs, w), blk)] * 4
            + [pl.BlockSpec((1, w), lambda b, p, c, lg: (0, 0))],
            out_specs=pl.BlockSpec((cs, w), lambda b, p, c, lg: (b * nc + p * c, 0)),
            scratch_shapes=[pltpu.VMEM((RET_HEADS, LANES, LANES), F32),
                            pltpu.VMEM((nc, RET_HEADS, LANES, LANES), F32)],
        ),
        out_shape=jax.ShapeDtypeStruct((n, w), BF16),
        compiler_params=_params(("arbitrary",) * 3),
    )(lg, q, k, v, g, gn)


def _mla_kernel(q_ref, k_ref, vt_ref, ot_ref, s_ref, p_ref, *, tk, nk, nchain):
    cols = q_ref.shape[0] // nchain
    qs = [q_ref[c * cols:(c + 1) * cols, :] for c in range(nchain)]

    def keys(j):
        return k_ref[pl.ds(pl.multiple_of(j * tk, tk), tk), :]

    def vals(j):
        return vt_ref[:, pl.ds(pl.multiple_of(j * tk, tk), tk)]

    def step(j, slot, carry):
        kn = keys(jnp.minimum(j + 1, nk - 1))
        vp = vals(jnp.maximum(j - 1, 0))
        out = []
        for c, (m, a_prev, acc) in enumerate(carry):
            s_ref[c, 1 - slot] = _dot_nt(kn, qs[c])
            acc = a_prev * acc + _dot(vp, p_ref[c, 1 - slot])
            s = s_ref[c, slot]
            m_new = jnp.maximum(m, jnp.max(s, axis=0, keepdims=True))
            p_ref[c, slot] = jnp.exp2(s - m_new).astype(BF16)
            out.append((m_new, jnp.exp2(m - m_new), acc))
        return tuple(out)

    k0 = keys(0)
    for c in range(nchain):
        s_ref[c, 0] = _dot_nt(k0, qs[c])
        p_ref[c, 1] = jnp.zeros(p_ref.shape[2:], BF16)
    init = tuple((jnp.full((1, cols), -jnp.inf, F32), jnp.ones((1, cols), F32),
                  jnp.zeros((LANES, cols), F32)) for _ in range(nchain))
    pairs = 2 if nk % 4 == 0 else 1

    def trip(t, cr):
        for u in range(pairs):
            j = 2 * (pairs * t + u)
            cr = step(j + 1, 1, step(j, 0, cr))
        return cr

    res = lax.fori_loop(0, nk // (2 * pairs), trip, init)
    vl = vals(nk - 1)
    for c, (_, a_prev, acc) in enumerate(res):
        acc = a_prev * acc + _dot(vl, p_ref[c, 1])
        ot_ref[:, c * cols:(c + 1) * cols] = (acc / acc[MLA_V:MLA_V + 1, :]).astype(BF16)


def _mla(q, k, vt, B, T, tq=512, tk=512, nchain=2):
    n = q.shape[0]
    tq = min(tq, T)
    tk = min(tk, T // 2)
    nq = T // tq
    nk = T // tk
    assert nk % 2 == 0
    cols = tq // nchain
    return pl.pallas_call(
        functools.partial(_mla_kernel, tk=tk, nk=nk, nchain=nchain),
        name="mla",
        grid=(B, MLA_HEADS, nq),
        in_specs=[
            pl.BlockSpec((tq, LANES), lambda b, h, i: (b * nq + i, h)),
            pl.BlockSpec((T, LANES), lambda b, h, i: (b, h)),
            pl.BlockSpec((LANES, T), lambda b, h, i: (h, b)),
        ],
        out_specs=pl.BlockSpec((LANES, tq), lambda b, h, i: (h, b * nq + i)),
        out_shape=jax.ShapeDtypeStruct((MLA_HEADS * LANES, n), BF16),
        scratch_shapes=[pltpu.VMEM((nchain, 2, tk, cols), F32), pltpu.VMEM((nchain, 2, tk, cols), BF16)],
        compiler_params=_params(("arbitrary",) * 3),
    )(q, k, vt)


def _outproj_kernel(*refs, npairs, transposed):
    x_ref = refs[0]
    ys = refs[1:1 + npairs]
    ws = refs[1 + npairs:1 + 2 * npairs]
    g_ref, rt_ref, xo_ref, h_ref, aff_ref = refs[1 + 2 * npairs:]
    half = x_ref.shape[0] // 2
    for r in (slice(0, half), slice(half, 2 * half)):
        acc = x_ref[r, :]
        for y, w, t in zip(ys, ws, transposed):
            acc = acc + (_dot_tn(y[:, r], w[...]) if t else _dot(y[r, :], w[...]))
        xo_ref[r, :] = acc
        h = _rms(acc, g_ref[...])
        h_hi = h.astype(BF16)
        h_ref[r, :] = h_hi
        part = _dot_nt(rt_ref[...], h_hi)
        h_lo = (h - h_hi.astype(F32)).astype(BF16)
        logits = (part[0:N_EXPERTS] + part[N_EXPERTS:2 * N_EXPERTS]
                  + _dot_nt(rt_ref[0:N_EXPERTS, :], h_lo))
        e = jnp.exp(logits - jnp.max(logits, axis=0, keepdims=True))
        aff_ref[:, r] = e / jnp.sum(e, axis=0, keepdims=True)


def _outproj_router(x, ys, ws, g, router_t, transposed=None, tm=512):
    n, d = x.shape
    transposed = tuple(transposed or (False,) * len(ys))
    row = lambda i: (i, 0)
    const = lambda i: (0, 0)
    return pl.pallas_call(
        functools.partial(_outproj_kernel, npairs=len(ys), transposed=transposed),
        name="outproj_router",
        grid=(n // tm,),
        in_specs=[pl.BlockSpec((tm, d), row)]
        + [pl.BlockSpec((y.shape[0], tm), lambda i: (0, i)) if t
           else pl.BlockSpec((tm, y.shape[1]), row) for y, t in zip(ys, transposed)]
        + [pl.BlockSpec(w.shape, const) for w in ws]
        + [pl.BlockSpec((1, d), const), pl.BlockSpec((2 * N_EXPERTS, d), const)],
        out_specs=[pl.BlockSpec((tm, d), row), pl.BlockSpec((tm, d), row),
                   pl.BlockSpec((N_EXPERTS, tm), lambda i: (0, i))],
        out_shape=[jax.ShapeDtypeStruct((n, d), F32), jax.ShapeDtypeStruct((n, d), BF16),
                   jax.ShapeDtypeStruct((N_EXPERTS, n), F32)],
        compiler_params=_params(("arbitrary",)),
    )(x, *ys, *ws, g, router_t)


def _rg_in_kernel(x_ref, g_ref, w_ref, gate_out, xr_out):
    hb = _rms(x_ref[...], g_ref[...]).astype(BF16)
    gate_out[...] = _dot(hb, w_ref[:, 0:D_RNN]).astype(BF16)
    xr_out[...] = _dot(hb, w_ref[:, D_RNN:2 * D_RNN])


def _rg_in(x, g, w, tm=256):
    n, d = x.shape
    row = lambda i: (i, 0)
    const = lambda i: (0, 0)
    return pl.pallas_call(
        _rg_in_kernel,
        name="rg_in",
        grid=(n // tm,),
        in_specs=[pl.BlockSpec((tm, d), row), pl.BlockSpec((1, d), const),
                  pl.BlockSpec((d, 2 * D_RNN), const)],
        out_specs=[pl.BlockSpec((tm, D_RNN), row), pl.BlockSpec((tm, D_RNN), row)],
        out_shape=[jax.ShapeDtypeStruct((n, D_RNN), BF16), jax.ShapeDtypeStruct((n, D_RNN), F32)],
        compiler_params=_params(("arbitrary",)),
    )(x, g, w)


HALO = 8
RG_LANES = 512
SUBLANES = 8


def _rg_scan_kernel(xr_ref, prev_ref, next_ref, gate_ref, cw_ref, cb_ref, wa_ref, ba_ref, wx_ref,
                    bx_ref, lam_ref, y_ref, carry_ref, hb_ref, *, tc, nt):
    p = pl.program_id(2)
    c = pl.program_id(3)
    chunk = p * c + (1 - p) * (nt - 1 - c)

    @pl.when(c == 0)
    def _():
        carry_ref[...] = jnp.zeros_like(carry_ref)

    cur = xr_ref[...]
    prev = jnp.where(chunk > 0, prev_ref[...], 0.0)
    nxt = jnp.where(chunk < nt - 1, next_ref[...], 0.0)
    ext = jnp.concatenate([prev, cur, nxt], axis=0)
    xc = cb_ref[...]
    for j in range(CONV_W):
        o = HALO - CONV_LEFT + j
        xc = xc + ext[o:o + tc, :] * cw_ref[j:j + 1, :]
    xcb = xc.astype(BF16)

    def block_diag(w_ref):
        return jnp.concatenate(
            [_dot(xcb[:, k * RG_BLK:(k + 1) * RG_BLK], w_ref[0, k])
             for k in range(RG_LANES // RG_BLK)], axis=1)

    r = jax.nn.sigmoid(block_diag(wa_ref) + ba_ref[0])
    gi = jax.nn.sigmoid(block_diag(wx_ref) + bx_ref[0])
    nl = -lam_ref[0]
    softplus = jnp.maximum(nl, 0.0) + jnp.log1p(jnp.exp(-jnp.abs(nl)))
    log_a = -RG_C * r * softplus
    a = jnp.exp(log_a)
    om = 1.0 - a * a
    b = jnp.where(om > 0.0, om * lax.rsqrt(om), 0.0) * (gi * xc)
    sub = lax.broadcasted_iota(I32, (tc, RG_LANES), 0) % SUBLANES

    def scan(a, b, reverse):
        groups = tc // SUBLANES
        shape3 = (groups, SUBLANES, RG_LANES)
        a = a.reshape(shape3)
        b = b.reshape(shape3)
        sub3 = sub.reshape(shape3)
        for s in (1, 2, 4):
            keep = (sub3 < SUBLANES - s) if reverse else (sub3 >= s)
            sh = SUBLANES - s if reverse else s
            a_sh = jnp.where(keep, pltpu.roll(a, sh, 1), 1.0)
            b_sh = jnp.where(keep, pltpu.roll(b, sh, 1), 0.0)
            b = a * b_sh + b
            a = a * a_sh
        a = a.reshape(tc, RG_LANES)
        b = b.reshape(tc, RG_LANES)
        out = [None] * groups
        carry = carry_ref[0:1, :]
        for g in (range(groups - 1, -1, -1) if reverse else range(groups)):
            rows = slice(g * SUBLANES, (g + 1) * SUBLANES)
            out[g] = a[rows] * carry + b[rows]
            carry = out[g][0:1] if reverse else out[g][SUBLANES - 1:SUBLANES]
        carry_ref[...] = jnp.broadcast_to(carry, carry_ref.shape)
        return jnp.concatenate(out, axis=0)

    @pl.when(p == 0)
    def _():
        hb_ref[chunk] = scan(a, b, True).astype(BF16)

    @pl.when(p == 1)
    def _():
        hs = scan(a, b, False) + hb_ref[chunk].astype(F32)
        y_ref[...] = (jax.nn.gelu(gate_ref[...].astype(F32)) * hs).astype(BF16)


def _rg_scan(xr, gate, cw, cb, wa, ba, wx, bx, lam, B, T, tc=256):
    n = xr.shape[0]
    tc = min(tc, T)
    nt = T // tc
    hpc = tc // HALO
    gb = RG_LANES // RG_BLK

    def chunk(p, c):
        return p * c + (1 - p) * (nt - 1 - c)

    cur = lambda b, j, p, c: (b * nt + chunk(p, c), j)
    prv = lambda b, j, p, c: (jnp.maximum((b * nt + chunk(p, c)) * hpc - 1, 0), j)
    nxt = lambda b, j, p, c: (jnp.minimum((b * nt + chunk(p, c) + 1) * hpc, n // HALO - 1), j)
    par = lambda b, j, p, c: (1 - p, j, 0, 0)
    vec = lambda b, j, p, c: (1 - p, 0, j)
    return pl.pallas_call(
        functools.partial(_rg_scan_kernel, tc=tc, nt=nt),
        name="rg_scan",
        grid=(B, D_RNN // RG_LANES, 2, nt),
        in_specs=[
            pl.BlockSpec((tc, RG_LANES), cur),
            pl.BlockSpec((HALO, RG_LANES), prv),
            pl.BlockSpec((HALO, RG_LANES), nxt),
            pl.BlockSpec((tc, RG_LANES), cur),
            pl.BlockSpec((CONV_W, RG_LANES), lambda b, j, p, c: (0, j)),
            pl.BlockSpec((1, RG_LANES), lambda b, j, p, c: (0, j)),
            pl.BlockSpec((1, gb, RG_BLK, RG_BLK), par),
            pl.BlockSpec((1, 1, RG_LANES), vec),
            pl.BlockSpec((1, gb, RG_BLK, RG_BLK), par),
            pl.BlockSpec((1, 1, RG_LANES), vec),
            pl.BlockSpec((1, 1, RG_LANES), vec),
        ],
        out_specs=pl.BlockSpec((tc, RG_LANES), lambda b, j, p, c: (b * nt + p * c, j)),
        out_shape=jax.ShapeDtypeStruct((n, D_RNN), BF16),
        scratch_shapes=[pltpu.VMEM((HALO, RG_LANES), F32), pltpu.VMEM((nt, tc, RG_LANES), BF16)],
        compiler_params=_params(("arbitrary",) * 4),
    )(xr, xr, xr, gate, cw, cb, wa, ba, wx, bx, lam)


SEL_BLK = 256


def _select_kernel(aff_ref, pos_ref, gate_ref, cs_ref, *, cap, nblk):
    keys = pltpu.bitcast(aff_ref[...], I32)

    def search(i, thr):
        cand = thr | lax.shift_left(jnp.int32(1), 30 - i)
        cnt = jnp.sum(jnp.where(keys >= cand, 1.0, 0.0), axis=1, keepdims=True)
        return jnp.where(cnt >= cap, cand, thr)

    thr = lax.fori_loop(0, 31, search, jnp.zeros((N_EXPERTS, 1), I32))
    need = cap - jnp.sum(jnp.where(keys > thr, 1.0, 0.0), axis=1, keepdims=True)
    ri = lax.broadcasted_iota(I32, (SEL_BLK, SEL_BLK), 0)
    ci = lax.broadcasted_iota(I32, (SEL_BLK, SEL_BLK), 1)
    tri = jnp.where(ri <= ci, 1.0, 0.0).astype(BF16)

    def body(j, carry):
        ceq, csel = carry
        off = pl.multiple_of(j * SEL_BLK, SEL_BLK)
        a = aff_ref[:, pl.ds(off, SEL_BLK)]
        kk = pltpu.bitcast(a, I32)
        gt = kk > thr
        eq = kk == thr
        eqf = jnp.where(eq, 1.0, 0.0)
        eqc = _dot(eqf.astype(BF16), tri) + ceq
        sel = gt | (eq & (eqc <= need))
        self_ = jnp.where(sel, 1.0, 0.0)
        selc = _dot(self_.astype(BF16), tri) + csel
        pos_ref[:, pl.ds(off, SEL_BLK)] = jnp.where(sel, selc - 1.0, -1.0).astype(I32)
        gate_ref[:, pl.ds(off, SEL_BLK)] = jnp.where(sel, a, 0.0)
        cs_ref[j] = jnp.broadcast_to(csel, (N_EXPERTS, LANES)).astype(I32)
        return (ceq + jnp.sum(eqf, axis=1, keepdims=True),
                csel + jnp.sum(self_, axis=1, keepdims=True))

    z = jnp.zeros((N_EXPERTS, 1), F32)
    lax.fori_loop(0, nblk, body, (z, z))


def _select(aff_t, cap):
    e, n = aff_t.shape
    nblk = n // SEL_BLK
    return pl.pallas_call(
        functools.partial(_select_kernel, cap=cap, nblk=nblk),
        name="moe_select",
        out_shape=[jax.ShapeDtypeStruct((e, n), I32), jax.ShapeDtypeStruct((e, n), F32),
                   jax.ShapeDtypeStruct((nblk, e, LANES), I32)],
        compiler_params=pltpu.CompilerParams(vmem_limit_bytes=VMEM_LIMIT),
    )(aff_t)


def _window(cs_ref, e, i):
    c = cs_ref[e, i]
    c1 = cs_ref[e, i + 1]
    shift = SLOT_ALIGN.bit_length() - 1
    b0 = lax.shift_left(lax.shift_right_logical(c, shift), shift)
    npass = lax.shift_right_logical(c1 - b0 + (SLOT_WIN - 1), SLOT_WIN.bit_length() - 1)
    return b0, npass


def _compact_kernel(cs_ref, h_ref, pos_ref, xg_ref, *, cap, tiles):
    e = pl.program_id(0)
    i = pl.program_id(1)

    @pl.when(i == 0)
    def _():
        xg_ref[...] = jnp.zeros_like(xg_ref)

    rows = lax.broadcasted_iota(I32, (SLOT_WIN, SEL_BLK), 0)

    def place(u, b0, k):
        lo = b0 + k * SLOT_WIN
        base = pl.multiple_of(jnp.minimum(lo, cap - SLOT_WIN), SLOT_ALIGN)
        pos = pos_ref[0, :, u * SEL_BLK:(u + 1) * SEL_BLK]
        tgt = jnp.where((pos >= lo) & (pos < lo + SLOT_WIN), pos - base, -1)
        onehot = jnp.where(rows == tgt, 1.0, 0.0).astype(BF16)
        x = h_ref[u * SEL_BLK:(u + 1) * SEL_BLK, :]
        xg_ref[0, pl.ds(base, SLOT_WIN), :] += _dot(onehot, x).astype(BF16)

    wins = [_window(cs_ref, e, i * tiles + u) for u in range(tiles)]
    for u, (b0, _) in enumerate(wins):
        place(u, b0, 0)
    for u, (b0, npass) in enumerate(wins):
        lax.fori_loop(1, npass, lambda k, c, u=u, b0=b0: (place(u, b0, k), c)[1], 0)


def _compact(cs, h, pos3, cap, tiles=8):
    n, d = h.shape
    tiles = min(tiles, n // SEL_BLK)
    blk = tiles * SEL_BLK
    return pl.pallas_call(
        functools.partial(_compact_kernel, cap=cap, tiles=tiles),
        name="moe_compact",
        grid_spec=pltpu.PrefetchScalarGridSpec(
            num_scalar_prefetch=1,
            grid=(N_EXPERTS, n // blk),
            in_specs=[pl.BlockSpec((blk, d), lambda e, i, cs: (i, 0)),
                      pl.BlockSpec((1, 1, blk), lambda e, i, cs: (e, 0, i))],
            out_specs=pl.BlockSpec((1, cap, d), lambda e, i, cs: (e, 0, 0)),
        ),
        out_shape=jax.ShapeDtypeStruct((N_EXPERTS, cap, d), BF16),
        compiler_params=_params(("arbitrary",) * 2),
    )(cs, h, pos3)


def _ffn_kernel(x_ref, wg_ref, wu_ref, wd_ref, o_ref, acc_ref):
    f = pl.program_id(2)

    @pl.when(f == 0)
    def _():
        acc_ref[...] = jnp.zeros_like(acc_ref)

    x = x_ref[0]
    a = _dot(x, wg_ref[0, 0].astype(BF16))
    hid = (a * jax.nn.sigmoid(a)) * _dot(x, wu_ref[0, 0].astype(BF16))
    acc_ref[...] += _dot(hid.astype(BF16), wd_ref[0, 0].astype(BF16))

    @pl.when(f == pl.num_programs(2) - 1)
    def _():
        o_ref[0] = acc_ref[...].astype(BF16)


def _ffn(xg, wg, wu, wd, layer, tm=1024, tf=1024):
    e, cap, d = xg.shape
    fdim = wg.shape[3]
    tm = min(tm, cap)
    return pl.pallas_call(
        _ffn_kernel,
        name="moe_ffn",
        grid=(e, cap // tm, fdim // tf),
        in_specs=[pl.BlockSpec((1, tm, d), lambda e, m, f: (e, m, 0)),
                  pl.BlockSpec((1, 1, d, tf), lambda e, m, f: (layer, e, 0, f)),
                  pl.BlockSpec((1, 1, d, tf), lambda e, m, f: (layer, e, 0, f)),
                  pl.BlockSpec((1, 1, tf, d), lambda e, m, f: (layer, e, f, 0))],
        out_specs=pl.BlockSpec((1, tm, d), lambda e, m, f: (e, m, 0)),
        out_shape=jax.ShapeDtypeStruct((e, cap, d), BF16),
        scratch_shapes=[pltpu.VMEM((tm, d), F32)],
        compiler_params=_params(("arbitrary",) * 3),
    )(xg, wg, wu, wd)


def _combine_kernel(cs_ref, x_ref, gate_ref, pos_ref, om_ref, xo_ref, buf, xbuf, sem, xsem,
                    *, cap, nt):
    i = pl.program_id(0)

    def win_copy(e, t, slot):
        b0, _ = _window(cs_ref, e, t)
        base = pl.multiple_of(jnp.minimum(b0, cap - SLOT_WIN), SLOT_ALIGN)
        return pltpu.make_async_copy(om_ref.at[e, pl.ds(base, SLOT_WIN), :], buf.at[slot, e],
                                     sem.at[slot, e])

    @pl.when(i == 0)
    def _():
        for e in range(N_EXPERTS):
            win_copy(e, 0, 0).start()

    @pl.when(i + 1 < nt)
    def _():
        for e in range(N_EXPERTS):
            win_copy(e, i + 1, (i + 1) % 2).start()

    slot = i % 2
    lane = lax.broadcasted_iota(I32, (SEL_BLK, SLOT_WIN), 1)

    ecol = lax.broadcasted_iota(I32, (1, N_EXPERTS), 1)
    lo_row = jnp.zeros((1, N_EXPERTS), I32)
    for e in range(N_EXPERTS):
        win_copy(e, i, slot).wait()
        lo_row = jnp.where(ecol == e, _window(cs_ref, e, i)[0], lo_row)
    pos_all = pos_ref[...]
    gate_all = gate_ref[...]
    tgt_all = jnp.where((pos_all >= lo_row) & (pos_all < lo_row + SLOT_WIN),
                        pos_all - jnp.minimum(lo_row, cap - SLOT_WIN), -1)
    onehot = jnp.concatenate(
        [jnp.where(lane == tgt_all[:, e:e + 1], gate_all[:, e:e + 1], 0.0).astype(BF16)
         for e in range(N_EXPERTS)], axis=1)
    rows = buf[slot].reshape(N_EXPERTS * SLOT_WIN, buf.shape[-1])
    xo_ref[...] = x_ref[...] + _dot(onehot, rows)

    for e in range(N_EXPERTS):
        b0, npass = _window(cs_ref, e, i)

        def extra(k, _, e=e, b0=b0):
            lo = b0 + k * SLOT_WIN
            base = pl.multiple_of(jnp.minimum(lo, cap - SLOT_WIN), SLOT_ALIGN)
            cp = pltpu.make_async_copy(om_ref.at[e, pl.ds(base, SLOT_WIN), :], xbuf, xsem)
            cp.start()
            cp.wait()
            pcol = pos_ref[:, e:e + 1]
            tgt = jnp.where((pcol >= lo) & (pcol < lo + SLOT_WIN), pcol - base, -1)
            onehot = jnp.where(lane == tgt, gate_ref[:, e:e + 1], 0.0).astype(BF16)
            xo_ref[...] += _dot(onehot, xbuf[...])
            return 0

        lax.fori_loop(1, npass, extra, 0)


def _combine(cs, x, gate_t, pos_t, om, cap):
    n, d = x.shape
    nt = n // SEL_BLK
    row = lambda i, cs: (i, 0)
    return pl.pallas_call(
        functools.partial(_combine_kernel, cap=cap, nt=nt),
        name="moe_combine",
        grid_spec=pltpu.PrefetchScalarGridSpec(
            num_scalar_prefetch=1,
            grid=(nt,),
            in_specs=[pl.BlockSpec((SEL_BLK, d), row),
                      pl.BlockSpec((SEL_BLK, N_EXPERTS), row),
                      pl.BlockSpec((SEL_BLK, N_EXPERTS), row),
                      pl.BlockSpec(memory_space=pl.ANY)],
            out_specs=pl.BlockSpec((SEL_BLK, d), row),
            scratch_shapes=[pltpu.VMEM((2, N_EXPERTS, SLOT_WIN, d), BF16),
                            pltpu.VMEM((SLOT_WIN, d), BF16),
                            pltpu.SemaphoreType.DMA((2, N_EXPERTS)),
                            pltpu.SemaphoreType.DMA(())],
        ),
        out_shape=jax.ShapeDtypeStruct((n, d), F32),
        compiler_params=_params(("arbitrary",)),
    )(cs, x, gate_t, pos_t, om)


def _moe(x, h, aff_t, wg, wu, wd, layer):
    n = x.shape[0]
    cap = max(1, EC_FACTOR * n // N_EXPERTS)
    pos, gate, cs3 = _select(aff_t, cap)
    cs = jnp.concatenate([cs3[:, :, 0].T, jnp.full((N_EXPERTS, 1), cap, I32)], axis=1)
    xg = _compact(cs, h, pos.reshape(N_EXPERTS, 1, n), cap)
    om = _ffn(xg, wg, wu, wd, layer)
    return _combine(cs, x, gate.T, pos.T, om, cap)


def _final_kernel(x_ref, g_ref, o_ref):
    o_ref[...] = _rms(x_ref[...], g_ref[...])


def _final_norm(x, g, tm=512):
    n, d = x.shape
    return pl.pallas_call(
        _final_kernel,
        name="final_norm",
        grid=(n // tm,),
        in_specs=[pl.BlockSpec((tm, d), lambda i: (i, 0)), pl.BlockSpec((1, d), lambda i: (0, 0))],
        out_specs=pl.BlockSpec((tm, d), lambda i: (i, 0)),
        out_shape=jax.ShapeDtypeStruct((n, d), F32),
        compiler_params=_params(("arbitrary",)),
    )(x, g)


def _rot_half_cols(w):
    half = w.shape[-1] // 2
    return jnp.concatenate([-w[..., half:], w[..., :half]], axis=-1)


def _pad_heads(w, heads, dim, lead=0):
    d = w.shape[0]
    w = w.reshape(d, heads, dim)
    w = jnp.pad(w, ((0, 0), (0, 0), (lead, LANES - lead - dim)))
    return w.reshape(d, heads * LANES)


def _prep_ab(w_in, w_uq, w_ukv, w_out):
    d = w_in.shape[0]
    cuts = [0, 256, 512, 1024, 1536, 1792, 1920, 1952]
    q_r, k_r, v_r, g_r, c_q, c_kv, k_pe = [w_in[:, a:b] for a, b in zip(cuts[:-1], cuts[1:])]
    q4 = q_r.reshape(d, RET_HEADS, RET_DK)
    k4 = k_r.reshape(d, RET_HEADS, RET_DK) * (RET_DK ** -0.5)
    ph = lambda w: _pad_heads(w.reshape(d, -1), RET_HEADS, RET_DK)
    kpe = jnp.pad(k_pe, ((0, 0), (MLA_NOPE, LANES - MLA_NOPE - MLA_ROPE)))
    kpe_rot = jnp.pad(_rot_half_cols(k_pe), ((0, 0), (MLA_NOPE, LANES - MLA_NOPE - MLA_ROPE)))
    w_ext = jnp.concatenate([ph(q4), ph(_rot_half_cols(q4)), ph(k4), ph(_rot_half_cols(k4)),
                             v_r, g_r, c_q, c_kv, kpe, kpe_rot], axis=1).astype(BF16)
    uq = w_uq.reshape(MLA_Q_LORA, MLA_HEADS, MLA_NOPE + MLA_ROPE)
    qa = jnp.pad(uq, ((0, 0), (0, 0), (0, LANES - MLA_NOPE - MLA_ROPE)))
    qb = jnp.pad(_rot_half_cols(uq[..., MLA_NOPE:]),
                 ((0, 0), (0, 0), (MLA_NOPE, LANES - MLA_NOPE - MLA_ROPE)))
    wuq = jnp.concatenate([qa.reshape(MLA_Q_LORA, -1), qb.reshape(MLA_Q_LORA, -1)], axis=1)
    ukv = w_ukv.reshape(MLA_KV_LORA, MLA_HEADS, MLA_NOPE + MLA_V)
    kn = jnp.pad(ukv[..., :MLA_NOPE], ((0, 0), (0, 0), (0, LANES - MLA_NOPE)))
    vv = jnp.pad(ukv[..., MLA_NOPE:], ((0, 0), (0, 0), (0, LANES - MLA_V)))
    wuk = kn.reshape(MLA_KV_LORA, -1)
    wvt = vv.reshape(MLA_KV_LORA, -1).T
    wo_a = w_out[:RET_HEADS * RET_DV]
    wo_b = w_out[RET_HEADS * RET_DV:].reshape(MLA_HEADS, MLA_V, d)
    wo_b = jnp.pad(wo_b, ((0, 0), (0, LANES - MLA_V), (0, 0))).reshape(MLA_HEADS * LANES, d)
    return (w_ext, wuq.astype(BF16), wuk.astype(BF16), wvt.astype(BF16), wo_a.astype(BF16),
            wo_b.astype(BF16))


def _rope_tabs(T):
    def tables(dim):
        inv = 1.0 / (ROPE_BASE ** (jnp.arange(0, dim, 2, dtype=F32) / dim))
        ang = jnp.arange(T, dtype=F32)[:, None] * inv[None, :]
        return jnp.cos(ang), jnp.sin(ang)

    cr, sr = tables(RET_DK)
    cr = jnp.tile(cr, (1, LANES // (RET_DK // 2)))
    sr = jnp.tile(sr, (1, LANES // (RET_DK // 2)))
    cm, sm = tables(MLA_ROPE)
    scale = (MLA_NOPE + MLA_ROPE) ** -0.5 * math.log2(math.e)
    ones = jnp.ones((T, MLA_NOPE), F32)
    z_lo = jnp.zeros((T, MLA_NOPE), F32)
    z_hi = jnp.zeros((T, LANES - MLA_NOPE - MLA_ROPE), F32)
    cq = jnp.concatenate([ones, cm, cm, z_hi], axis=1) * scale
    sq = jnp.concatenate([z_lo, sm, sm, z_hi], axis=1) * scale
    ck = jnp.concatenate([z_lo, cm, cm, z_hi], axis=1)
    sk = jnp.concatenate([z_lo, sm, sm, z_hi], axis=1)
    return cr, sr, cq, sq, ck, sk


def _trunk(x3, p):
    B, T, d = x3.shape
    x = x3.reshape(B * T, d)
    tabs = _rope_tabs(T)
    for l in range(DEPTH):
        j = l // 2
        g_mix = p["norm_mix"][l].reshape(1, d)
        g_ffn = p["norm_ffn"][l].reshape(1, d)
        rt = p["moe_router"][l].T
        rt_hi = rt.astype(BF16)
        router_t = jnp.concatenate([rt_hi, (rt - rt_hi.astype(F32)).astype(BF16)], axis=0)
        if l % 2 == 0:
            w_ext, wuq, wuk, wvt, wo_a, wo_b = p["ab"][j]
            qr, kr, vr, gr, q, k, vt = _ab_in(x, g_mix, w_ext, wuq, wuk, wvt,
                                             p["mla_q_norm"][j].reshape(1, -1),
                                             p["mla_kv_norm"][j].reshape(1, -1), tabs, T)
            lg = jax.nn.log_sigmoid(p["ret_decay_logit"][j].astype(F32))
            y_a = _retention(lg, qr, kr, vr, gr, p["ret_gn"][j].reshape(1, -1), B, T)
            y_bt = _mla(q, k, vt, B, T)
            x, h, aff_t = _outproj_router(x, [y_a, y_bt], [wo_a, wo_b], g_ffn, router_t,
                                          transposed=(False, True))
        else:
            gate, xr = _rg_in(x, g_mix, p["rg_w_in"][j])
            y = _rg_scan(xr, gate, p["rg_conv_w"][j], p["rg_conv_b"][j].reshape(1, -1),
                         p["rg_wa"][j], p["rg_ba"][j].reshape(2, 1, -1), p["rg_wx"][j],
                         p["rg_bx"][j].reshape(2, 1, -1), p["rg_lambda"][j].reshape(2, 1, -1), B, T)
            x, h, aff_t = _outproj_router(x, [y], [p["rg_w_out"][j]], g_ffn, router_t)
        x = _moe(x, h, aff_t, p["moe_w_gate"], p["moe_w_up"], p["moe_w_down"], l)
    return _final_norm(x, p["norm_final"].reshape(1, d)).reshape(B, T, d)


def kernel(x_prompt, x_sample, norm_mix, norm_ffn, norm_final, ab_w_in, ret_decay_logit, ret_gn, mla_q_norm, mla_w_uq, mla_kv_norm, mla_w_ukv, ab_w_out, rg_w_in, rg_conv_w, rg_conv_b, rg_wa, rg_ba, rg_wx, rg_bx, rg_lambda, rg_w_out, moe_router, moe_w_gate, moe_w_up, moe_w_down):
    p = dict(
        norm_mix=norm_mix, norm_ffn=norm_ffn, norm_final=norm_final,
        ret_decay_logit=ret_decay_logit, ret_gn=ret_gn, mla_q_norm=mla_q_norm,
        mla_kv_norm=mla_kv_norm, rg_conv_w=rg_conv_w, rg_conv_b=rg_conv_b, rg_ba=rg_ba,
        rg_bx=rg_bx, rg_lambda=rg_lambda, moe_router=moe_router,
        ab=[_prep_ab(ab_w_in[j], mla_w_uq[j], mla_w_ukv[j], ab_w_out[j])
            for j in range(ab_w_in.shape[0])],
        rg_w_in=rg_w_in.astype(BF16), rg_wa=rg_wa.astype(BF16), rg_wx=rg_wx.astype(BF16),
        rg_w_out=rg_w_out.astype(BF16), moe_w_gate=moe_w_gate, moe_w_up=moe_w_up,
        moe_w_down=moe_w_down,
    )
    return _trunk(x_prompt, p), _trunk(x_sample, p)
```
